```python
import math
import jax, jax.numpy as jnp
from jax import lax
import numpy as np

D_MODEL = 1024
BATCH = 32
SEQ = 2048
DEPTH = 1
DEC_BATCH = 128
DEC_SEQ = 8
PAST_LEN = 16384
PAGE_SIZE = 128

MLA_HEADS = 16
QK_NOPE = 64
QK_ROPE = 32
QK_DIM = QK_NOPE + QK_ROPE
V_DIM = 64
Q_LORA = 384
KV_LORA = 256
ROPE_THETA = 10000.0
Q_BLOCK = 128
MLA_WIDTH = MLA_HEADS * V_DIM
SCALE = QK_DIM ** -0.5
RW_HEADS = 16
RW_HEAD = 64
RW_WIDTH = RW_HEADS * RW_HEAD
DECAY_LORA = 64
A_LORA = 64
N_SHIFT = 3 * RW_WIDTH + DECAY_LORA + A_LORA
RMS_EPS = 1e-6
GN_EPS = 64e-5
SPLITS = (Q_LORA, KV_LORA, QK_ROPE, MLA_WIDTH, N_SHIFT, RW_WIDTH, D_MODEL, D_MODEL)
N_IN = Q_LORA + KV_LORA + QK_ROPE + MLA_WIDTH + N_SHIFT + RW_WIDTH + 2 * D_MODEL

kernel_name = "mla_rwkv7_gated_hybrid_step"


def _rms(x):
    xf = x.astype(jnp.float32)
    return (xf * lax.rsqrt(jnp.mean(xf * xf, -1, keepdims=True) + RMS_EPS)).astype(x.dtype)


def _rope(x, pos):
    half = QK_ROPE // 2
    inv = ROPE_THETA ** (-jnp.arange(half, dtype=jnp.float32) / half)
    ang = pos.astype(jnp.float32)[:, None] * inv[None, :]
    shape = (1, ang.shape[0]) + (1,) * (x.ndim - 3) + (half,)
    cos = jnp.cos(ang).reshape(shape).astype(x.dtype)
    sin = jnp.sin(ang).reshape(shape).astype(x.dtype)
    x1, x2 = x[..., :half], x[..., half:]
    return jnp.concatenate([x1 * cos - x2 * sin, x1 * sin + x2 * cos], -1)


def _split(cols):
    out, i = [], 0
    for s in SPLITS:
        out.append(cols[..., i:i + s])
        i += s
    return out


def _front(x, c, w_ada, b_ada, g_norm, w_in):
    mod = jnp.dot(jax.nn.silu(c), w_ada) + b_ada
    shift, scale, gate = jnp.split(mod, 3, axis=-1)
    h = _rms(x) * g_norm * (1 + scale[:, None, :]) + shift[:, None, :]
    cols = jnp.einsum('btd,dn->btn', h, w_in)
    return _split(cols), gate


def _mla_query_latent(qa, kva, pos, g_qa, w_uq, g_kva, g_q):
    q = jnp.einsum('btr,rhd->bthd', _rms(qa) * g_qa, w_uq)
    q = _rms(q) * g_q
    q = jnp.concatenate([q[..., :QK_NOPE], _rope(q[..., QK_NOPE:], pos)], -1)
    c_lat = _rms(kva) * g_kva
    return q, c_lat


def _mla_keys(c_lat, kpe_raw, pos, w_uk, g_k):
    k_nope = jnp.einsum('btr,rhd->bthd', c_lat, w_uk)
    kn = k_nope.astype(jnp.float32)
    kp = kpe_raw.astype(jnp.float32)
    ms = (jnp.sum(kn * kn, -1) + jnp.sum(kp * kp, -1)[..., None]) / QK_DIM
    k_inv = lax.rsqrt(ms + RMS_EPS).astype(c_lat.dtype)
    k_pe = _rope(kpe_raw * g_k[QK_NOPE:], pos)
    return k_nope, k_pe, k_inv


def _prompt_attention(q, k, v):
    B, S, H, _ = q.shape
    nb = S // Q_BLOCK
    qb = jnp.moveaxis(q.reshape(B, nb, Q_BLOCK, H, QK_DIM), 1, 0)
    kpos = jnp.arange(S)

    def blk(args):
        qi, i = args
        s = jnp.einsum('bqhd,bkhd->bhqk', qi, k).astype(jnp.float32) * SCALE
        qpos = i * Q_BLOCK + jnp.arange(Q_BLOCK)
        s = jnp.where(kpos[None, :] <= qpos[:, None], s, -jnp.inf)
        p = jax.nn.softmax(s, axis=-1).astype(v.dtype)
        return jnp.einsum('bhqk,bkhd->bqhd', p, v)

    o = lax.map(blk, (qb, jnp.arange(nb)))
    return jnp.moveaxis(o, 0, 1).reshape(B, S, H * V_DIM)


def _sample_attention(q, c_new, kpe_new, kinv_new, cache_kv, cache_kpe, cache_kinv, page_table, w_uk, w_uv, g_k):
    Bd, T = q.shape[:2]
    L = page_table.shape[1] * PAGE_SIZE
    q_lat = jnp.einsum('bthd,rhd->bthr', q[..., :QK_NOPE] * g_k[:QK_NOPE], w_uk)
    q_pe = q[..., QK_NOPE:]
    allowed = jnp.concatenate([jnp.ones((T, L), bool), jnp.tril(jnp.ones((T, T), bool))], axis=1)

    def one(args):
        pt, ql, qp, cn, kn, sn = args
        c_all = jnp.concatenate([cache_kv[pt].reshape(L, KV_LORA).astype(cn.dtype), cn], 0)
        kp_all = jnp.concatenate([cache_kpe[pt].reshape(L, QK_ROPE).astype(kn.dtype), kn], 0)
        ki_all = jnp.concatenate([cache_kinv[pt].reshape(L, MLA_HEADS).astype(sn.dtype), sn], 0)
        s = (jnp.einsum('thr,kr->htk', ql, c_all) + jnp.einsum('thr,kr->htk', qp, kp_all)).astype(jnp.float32)
        s = s * ki_all.T[:, None, :].astype(jnp.float32) * SCALE
        s = jnp.where(allowed[None], s, -jnp.inf)
        p = jax.nn.softmax(s, axis=-1).astype(c_all.dtype)
        o_lat = jnp.einsum('htk,kr->thr', p, c_all)
        return jnp.einsum('thr,rhd->thd', o_lat, w_uv)

    o = lax.map(one, (page_table, q_lat, q_pe, c_new, kpe_new, kinv_new))
    return o.reshape(Bd, T, MLA_HEADS * V_DIM)


def _wkv_scan(S0, r, decay, k, v, kk, a):
    def step(S, inp):
        r_t, d_t, k_t, v_t, kk_t, a_t = inp
        Sk = jnp.einsum('bhij,bhj->bhi', S, kk_t)
        S = S * d_t[:, :, None, :] - Sk[..., None] * (kk_t * a_t)[:, :, None, :] + v_t[..., None] * k_t[:, :, None, :]
        return S, jnp.einsum('bhij,bhj->bhi', S, r_t)

    xs = tuple(jnp.moveaxis(t.astype(jnp.float32), 1, 0) for t in (r, decay, k, v, kk, a))
    S, o = lax.scan(step, S0.astype(jnp.float32), xs)
    return S, jnp.moveaxis(o, 0, 1)


def _rwkv(p, z, shift0, S0, mu_shift, w0, w_decay2, a0, w_a2, k_k, k_a, r_k, gn_g, gn_b):
    B, T, _ = p.shape
    prev = jnp.concatenate([shift0[:, None, :].astype(p.dtype), p[:, :-1]], axis=1)
    xs = p + (prev - p) * mu_shift
    o1 = RW_WIDTH
    o2 = o1 + DECAY_LORA
    o3 = o2 + RW_WIDTH
    o4 = o3 + RW_WIDTH
    r, wl, k, v, al = xs[..., :o1], xs[..., o1:o2], xs[..., o2:o3], xs[..., o3:o4], xs[..., o4:]
    w = -jax.nn.softplus(-(w0 + jnp.tanh(wl) @ w_decay2)) - 0.5
    decay = jnp.exp(-jnp.exp(w.astype(jnp.float32)))
    a = jax.nn.sigmoid(a0 + al @ w_a2)
    hs = lambda t: t.reshape(B, T, RW_HEADS, RW_HEAD)
    r, k, v, a, decay = hs(r), hs(k), hs(v), hs(a), hs(decay)
    kk = (k * k_k.reshape(RW_HEADS, RW_HEAD)).astype(jnp.float32)
    kk = kk * lax.rsqrt(jnp.maximum(jnp.sum(kk * kk, -1, keepdims=True), 1e-24))
    k = k * (1 + (a - 1) * k_a.reshape(RW_HEADS, RW_HEAD))
    S, o = _wkv_scan(S0, r, decay, k, v, kk, a)
    mean = jnp.mean(o, -1, keepdims=True)
    var = jnp.mean(jnp.square(o - mean), -1, keepdims=True)
    o = ((o - mean) * lax.rsqrt(var + GN_EPS)).astype(p.dtype)
    o = o * gn_g.reshape(RW_HEADS, RW_HEAD) + gn_b.reshape(RW_HEADS, RW_HEAD)
    o = o + jnp.sum(r * k * r_k, -1, keepdims=True) * v
    o = o.reshape(B, T, RW_WIDTH) * jax.nn.silu(z)
    return o, S, p[:, -1]


def _merge(x, gate, o_a, o_b, g_a, g_b, w_pa, w_pb, w_out):
    m = jax.nn.sigmoid(g_a) * (o_a @ w_pa) + jax.nn.sigmoid(g_b) * (o_b @ w_pb)
    return x + gate[:, None, :] * (m @ w_out)


def setup_inputs(seed: int = 0) -> dict:
    key = jax.random.key(seed)
    ks = list(jax.random.split(key, 40))
    nrm = lambda shape, s: jax.random.normal(ks.pop(), shape, jnp.float32) * s
    n_pages = PAST_LEN // PAGE_SIZE
    n_phys = (5 * DEC_BATCH * n_pages) // 4
    perm = jax.random.permutation(ks.pop(), n_phys)
    page_table = perm[:DEC_BATCH * n_pages].reshape(DEC_BATCH, n_pages).astype(jnp.int32)
    return {
        "x_prompt": nrm((BATCH, SEQ, D_MODEL), 1.0),
        "x_sample": nrm((DEC_BATCH, DEC_SEQ, D_MODEL), 1.0),
        "c_prompt": nrm((BATCH, D_MODEL), 1.0),
        "c_sample": nrm((DEC_BATCH, D_MODEL), 1.0),
        "cache_kv": nrm((n_phys, PAGE_SIZE, KV_LORA), 1.0),
        "cache_kpe": nrm((n_phys, PAGE_SIZE, QK_ROPE), 1.0),
        "cache_kinv": jax.random.uniform(ks.pop(), (n_phys, PAGE_SIZE, MLA_HEADS), jnp.float32, 0.7, 1.3),
        "state_wkv": nrm((DEC_BATCH, RW_HEADS, RW_HEAD, RW_HEAD), 0.3),
        "state_shift": nrm((DEC_BATCH, N_SHIFT), 1.0),
        "page_table": page_table,
        "w_ada": nrm((D_MODEL, 3 * D_MODEL), 0.5 * D_MODEL ** -0.5),
        "b_ada": nrm((3 * D_MODEL,), 0.01),
        "g_norm": 1.0 + nrm((D_MODEL,), 0.01),
        "w_in": nrm((D_MODEL, N_IN), D_MODEL ** -0.5),
        "g_qa": 1.0 + nrm((Q_LORA,), 0.01),
        "w_uq": nrm((Q_LORA, MLA_HEADS, QK_DIM), Q_LORA ** -0.5),
        "g_kva": 1.0 + nrm((KV_LORA,), 0.01),
        "w_uk": nrm((KV_LORA, MLA_HEADS, QK_NOPE), KV_LORA ** -0.5),
        "w_uv": nrm((KV_LORA, MLA_HEADS, V_DIM), KV_LORA ** -0.5),
        "g_q": 1.0 + nrm((QK_DIM,), 0.01),
        "g_k": 1.0 + nrm((QK_DIM,), 0.01),
        "w_pa": nrm((MLA_WIDTH, D_MODEL), MLA_WIDTH ** -0.5),
        "mu_shift": jax.random.uniform(ks.pop(), (N_SHIFT,), jnp.float32),
        "w0": jax.random.uniform(ks.pop(), (RW_WIDTH,), jnp.float32, -3.0, 1.0),
        "w_decay2": nrm((DECAY_LORA, RW_WIDTH), 0.1),
        "a0": nrm((RW_WIDTH,), 0.1),
        "w_a2": nrm((A_LORA, RW_WIDTH), 0.1),
        "k_k": 1.0 + nrm((RW_WIDTH,), 0.1),
        "k_a": 1.0 + nrm((RW_WIDTH,), 0.1),
        "r_k": nrm((RW_HEADS, RW_HEAD), 0.1),
        "gn_g": 1.0 + nrm((RW_WIDTH,), 0.01),
        "gn_b": nrm((RW_WIDTH,), 0.01),
        "w_pb": nrm((RW_WIDTH, D_MODEL), RW_WIDTH ** -0.5),
        "w_out": nrm((D_MODEL, D_MODEL), D_MODEL ** -0.5),
    }


def reference(x_prompt, x_sample, c_prompt, c_sample, cache_kv, cache_kpe, cache_kinv, state_wkv, state_shift,
              page_table, w_ada, b_ada, g_norm, w_in, g_qa, w_uq, g_kva, w_uk, w_uv, g_q, g_k, w_pa, mu_shift, w0,
              w_decay2, a0, w_a2, k_k, k_a, r_k, gn_g, gn_b, w_pb, w_out):
    B, S, _ = x_prompt.shape
    Bd, T, _ = x_sample.shape
    past = page_table.shape[1] * PAGE_SIZE

    pos_p = jnp.arange(S)
    (qa, kva, kpe_raw, z_a, p_sh, z_b, g_a, g_b), gate = _front(x_prompt, c_prompt, w_ada, b_ada, g_norm, w_in)
    q, kv_p = _mla_query_latent(qa, kva, pos_p, g_qa, w_uq, g_kva, g_q)
    k_nope, kpe_p, kinv_p = _mla_keys(kv_p, kpe_raw, pos_p, w_uk, g_k)
    keys = jnp.concatenate([k_nope * g_k[:QK_NOPE],
                            jnp.broadcast_to(kpe_p[:, :, None, :], (B, S, MLA_HEADS, QK_ROPE))], -1) * kinv_p[..., None]
    vals = jnp.einsum('btr,rhd->bthd', kv_p, w_uv)
    o_a = _prompt_attention(q, keys, vals) * jax.nn.silu(z_a)
    o_b, wkv_p, shift_p = _rwkv(p_sh, z_b, jnp.zeros((B, N_SHIFT), x_prompt.dtype),
                                jnp.zeros((B, RW_HEADS, RW_HEAD, RW_HEAD), jnp.float32),
                                mu_shift, w0, w_decay2, a0, w_a2, k_k, k_a, r_k, gn_g, gn_b)
    y_prompt = _merge(x_prompt, gate, o_a, o_b, g_a, g_b, w_pa, w_pb, w_out)

    pos_s = past + jnp.arange(T)
    (qa_s, kva_s, kpe_raw_s, z_a_s, p_sh_s, z_b_s, g_a_s, g_b_s), gate_s = _front(x_sample, c_sample, w_ada, b_ada,
                                                                                 g_norm, w_in)
    q_s, kv_s = _mla_query_latent(qa_s, kva_s, pos_s, g_qa, w_uq, g_kva, g_q)
    _, kpe_s, kinv_s = _mla_keys(kv_s, kpe_raw_s, pos_s, w_uk, g_k)
    o_a_s = _sample_attention(q_s, kv_s, kpe_s, kinv_s, cache_kv, cache_kpe, cache_kinv, page_table,
                              w_uk, w_uv, g_k) * jax.nn.silu(z_a_s)
    o_b_s, wkv_s, shift_s = _rwkv(p_sh_s, z_b_s, state_shift, state_wkv,
                                  mu_shift, w0, w_decay2, a0, w_a2, k_k, k_a, r_k, gn_g, gn_b)
    y_sample = _merge(x_sample, gate_s, o_a_s, o_b_s, g_a_s, g_b_s, w_pa, w_pb, w_out)

    return (y_prompt, y_sample, kv_p, kpe_p, kinv_p, wkv_p.astype(x_prompt.dtype), shift_p,
            kv_s, kpe_s, kinv_s, wkv_s.astype(x_sample.dtype), shift_s)
```

```python
import functools
import math

import jax
import jax.numpy as jnp
from jax import lax
from jax.experimental import pallas as pl
from jax.experimental.pallas import tpu as pltpu

F32 = jnp.float32
BF16 = jnp.bfloat16

D_MODEL = 1024
PAGE_SIZE = 128
MLA_HEADS = 16
QK_NOPE = 64
QK_ROPE = 32
QK_DIM = QK_NOPE + QK_ROPE
V_DIM = 64
Q_LORA = 384
KV_LORA = 256
ROPE_THETA = 10000.0
MLA_WIDTH = MLA_HEADS * V_DIM
SCALE = QK_DIM ** -0.5
RW_HEADS = 16
RW_HEAD = 64
RW_WIDTH = RW_HEADS * RW_HEAD
DECAY_LORA = 64
A_LORA = 64
RMS_EPS = 1e-6
GN_EPS = 64e-5

LANES = 128
HEAD_PAD = LANES
QKV_WIDTH = MLA_HEADS * HEAD_PAD
NOPE_OFF = HEAD_PAD - QK_NOPE
HALF_ROPE = QK_ROPE // 2
WKV_CHUNK = 64
WKV_GROUP = 4
GROUP_LANES = WKV_GROUP * RW_HEAD
VMEM_LIMIT = 56 * 1024 * 1024
NEG_BIG = -1e30
DECAY_SCALE = math.exp(-0.5)


def _dot(a, b):
    return jnp.dot(a.astype(BF16), b.astype(BF16), preferred_element_type=F32)


def _dot_nt(a, b):
    return lax.dot_general(a.astype(BF16), b.astype(BF16), (((1,), (1,)), ((), ())), preferred_element_type=F32)


def _split2(x):
    hi = x.astype(BF16)
    lo = (x - hi.astype(F32)).astype(BF16)
    return hi, lo


def _split3(x):
    hi = x.astype(BF16)
    r1 = x - hi.astype(F32)
    mid = r1.astype(BF16)
    lo = (r1 - mid.astype(F32)).astype(BF16)
    return hi, mid, lo


def _sigmoid(x):
    return 1.0 / (1.0 + jnp.exp(-x))


def _silu(x):
    return x * _sigmoid(x)


def _rms(x, eps=RMS_EPS):
    return x * lax.rsqrt(jnp.mean(x * x, axis=-1, keepdims=True) + eps)


def _params(*sem):
    return pltpu.CompilerParams(dimension_semantics=sem, vmem_limit_bytes=VMEM_LIMIT)


def _const_spec(shape):
    nd = len(shape)
    return pl.BlockSpec(shape, lambda *_: (0,) * nd)


def _mod_kernel(c_ref, w_ref, b_ref, o_ref):
    o_ref[...] = _dot(_silu(c_ref[...]), w_ref[...]) + b_ref[...]


def _mod(c, w_ada, b_ada):
    n = c.shape[0]
    return pl.pallas_call(
        _mod_kernel,
        out_shape=jax.ShapeDtypeStruct((n, 3 * D_MODEL), F32),
        name="mod",
        compiler_params=pltpu.CompilerParams(vmem_limit_bytes=VMEM_LIMIT),
    )(c, w_ada, b_ada)


def _modulated(x_ref, mod_ref, gn_ref, bb, tt):
    x = x_ref[...]
    mod = mod_ref[...]
    shift = mod[:, :, :D_MODEL]
    scale = mod[:, :, D_MODEL:2 * D_MODEL]
    h = _rms(x) * gn_ref[...] * (1.0 + scale) + shift
    return h.reshape(bb * tt, D_MODEL).astype(BF16)


def _rope(x, cos, sup, sdn):
    return x * cos + pltpu.roll(x, HALF_ROPE, 1) * sup + pltpu.roll(x, HEAD_PAD - HALF_ROPE, 1) * sdn


def _front_mla_kernel(x_ref, mod_ref, gn_ref, cos_ref, sup_ref, sdn_ref,
                      wqa_ref, wkva_ref, wkpe_ref, wza_ref, wga_ref, wgb_ref,
                      gqa_ref, wuq_ref, gq_ref, gkva_ref, wuk_ref, gkr_ref, gkn_ref, wuv_ref,
                      *out_refs, bb, tt, emit_kv):
    if emit_kv:
        q_ref, kv_ref, kpe_ref, kinv_ref, za_ref, ga_ref, gb_ref, k_ref, v_ref = out_refs
    else:
        q_ref, kv_ref, kpe_ref, kinv_ref, za_ref, ga_ref, gb_ref = out_refs
    tm = bb * tt
    hb = _modulated(x_ref, mod_ref, gn_ref, bb, tt)

    def table(ref):
        return jnp.broadcast_to(ref[...][None], (bb, tt, HEAD_PAD)).reshape(tm, HEAD_PAD)

    cos, sup, sdn = table(cos_ref), table(sup_ref), table(sdn_ref)

    za_ref[...] = _dot(hb, wza_ref[...]).reshape(bb, tt, MLA_WIDTH)
    ga_ref[...] = _dot(hb, wga_ref[...]).reshape(bb, tt, D_MODEL)
    gb_ref[...] = _dot(hb, wgb_ref[...]).reshape(bb, tt, D_MODEL)

    qn = _rms(_dot(hb, wqa_ref[...])) * gqa_ref[...]
    qf = _dot(qn, wuq_ref[...])
    gq = gq_ref[...]
    for h in range(MLA_HEADS):
        sl = slice(h * HEAD_PAD, (h + 1) * HEAD_PAD)
        qh = qf[:, sl]
        ms = jnp.sum(qh * qh, axis=-1, keepdims=True) * (1.0 / QK_DIM)
        qh = _rope(qh * lax.rsqrt(ms + RMS_EPS) * gq, cos, sup, sdn)
        q_ref[:, :, sl] = qh.reshape(bb, tt, HEAD_PAD).astype(q_ref.dtype)

    c_lat = _rms(_dot(hb, wkva_ref[...])) * gkva_ref[...]
    kv_ref[...] = c_lat.reshape(bb, tt, KV_LORA)
    cb = c_lat.astype(BF16)
    kpe_raw = _dot(hb, wkpe_ref[...])
    kp2 = jnp.sum(kpe_raw * kpe_raw, axis=-1, keepdims=True)
    kpe_rot = _rope(kpe_raw * gkr_ref[...], cos, sup, sdn)
    kpe_ref[...] = kpe_rot[:, :QK_ROPE].reshape(bb, tt, QK_ROPE)

    knf = _dot(cb, wuk_ref[...])
    gkn = gkn_ref[...]
    head_lane = lax.broadcasted_iota(jnp.int32, (tm, MLA_HEADS), 1)
    kinv_all = jnp.zeros((tm, MLA_HEADS), F32)
    for h in range(MLA_HEADS):
        sl = slice(h * HEAD_PAD, (h + 1) * HEAD_PAD)
        kn = knf[:, sl]
        ms = (jnp.sum(kn * kn, axis=-1, keepdims=True) + kp2) * (1.0 / QK_DIM)
        kinv = lax.rsqrt(ms + RMS_EPS)
        kinv_all = jnp.where(head_lane == h, kinv, kinv_all)
        if emit_kv:
            k_ref[:, :, sl] = ((kn * gkn + kpe_rot) * kinv).reshape(bb, tt, HEAD_PAD).astype(BF16)
    kinv_ref[...] = kinv_all.reshape(bb, tt, MLA_HEADS)
    if emit_kv:
        v_ref[...] = _dot(cb, wuv_ref[...]).reshape(bb, tt, QKV_WIDTH).astype(BF16)


def _front_mla(x, mod, tabs, w, *, bb, tt, q_dtype, emit_kv):
    nb, t_all, _ = x.shape
    grid = (nb // bb, t_all // tt)
    tok = lambda width: pl.BlockSpec((bb, tt, width), lambda i, j: (i, j, 0))
    tab = pl.BlockSpec((tt, HEAD_PAD), lambda i, j: (j, 0))
    weights = [w["w_qa"], w["w_kva"], w["w_kpe"], w["w_za"], w["w_ga"], w["w_gb"], w["g_qa"], w["w_uq"], w["g_q"],
               w["g_kva"], w["w_uk"], w["g_k_rope"], w["g_k_nope"], w["w_uv"]]
    in_specs = ([tok(D_MODEL), pl.BlockSpec((bb, 1, 3 * D_MODEL), lambda i, j: (i, 0, 0)),
                 _const_spec((1, 1, D_MODEL)), tab, tab, tab] + [_const_spec(a.shape) for a in weights])
    sds = lambda width, dt=F32: jax.ShapeDtypeStruct((nb, t_all, width), dt)
    out_shape = [sds(QKV_WIDTH, q_dtype), sds(KV_LORA), sds(QK_ROPE), sds(MLA_HEADS), sds(MLA_WIDTH), sds(D_MODEL),
                 sds(D_MODEL)]
    out_specs = [tok(QKV_WIDTH), tok(KV_LORA), tok(QK_ROPE), tok(MLA_HEADS), tok(MLA_WIDTH), tok(D_MODEL),
                 tok(D_MODEL)]
    if emit_kv:
        out_shape += [sds(QKV_WIDTH, BF16), sds(QKV_WIDTH, BF16)]
        out_specs += [tok(QKV_WIDTH), tok(QKV_WIDTH)]
    return pl.pallas_call(
        functools.partial(_front_mla_kernel, bb=bb, tt=tt, emit_kv=emit_kv),
        grid=grid, in_specs=in_specs, out_specs=out_specs, out_shape=out_shape,
        name="front_mla", compiler_params=_params("parallel", "arbitrary"),
    )(x, mod, w["g_norm"], *tabs, *weights)


def _front_rwkv_kernel(x_ref, mod_ref, gn_ref, s0r_ref, s0w_ref, s0k_ref, s0v_ref, s0a_ref,
                       mur_ref, muw_ref, muk_ref, muv_ref, mua_ref,
                       wr_ref, ww_ref, wk_ref, wv_ref, wa_ref, wzb_ref,
                       wd2_ref, w0_ref, wa2_ref, a0_ref,
                       r_ref, k_ref, v_ref, ld_ref, a_ref, zb_ref,
                       lr_ref, lw_ref, lk_ref, lv_ref, la_ref,
                       cr_ref, cw_ref, ck_ref, cv_ref, ca_ref, *, bb, tt):
    j = pl.program_id(1)
    hb = _modulated(x_ref, mod_ref, gn_ref, bb, tt)

    def shifted(w_ref, s0_ref, mu_ref, last_ref, carry_ref):
        width = w_ref.shape[1]
        p = _dot(hb, w_ref[...]).reshape(bb, tt, width)
        first = jnp.where(j == 0, s0_ref[...], carry_ref[...])
        t_idx = lax.broadcasted_iota(jnp.int32, (bb, tt, width), 1)
        prev = jnp.where(t_idx == 0, first, pltpu.roll(p, 1, 1))
        last = p[:, tt - 1:tt, :]
        last_ref[...] = last
        carry_ref[...] = last
        return (p + (prev - p) * mu_ref[...]).reshape(bb * tt, width)

    r_ref[...] = shifted(wr_ref, s0r_ref, mur_ref, lr_ref, cr_ref).reshape(bb, tt, RW_WIDTH)
    k_ref[...] = shifted(wk_ref, s0k_ref, muk_ref, lk_ref, ck_ref).reshape(bb, tt, RW_WIDTH)
    v_ref[...] = shifted(wv_ref, s0v_ref, muv_ref, lv_ref, cv_ref).reshape(bb, tt, RW_WIDTH)
    wl = shifted(ww_ref, s0w_ref, muw_ref, lw_ref, cw_ref)
    al = shifted(wa_ref, s0a_ref, mua_ref, la_ref, ca_ref)
    y = w0_ref[...] + _dot(jnp.tanh(wl), wd2_ref[...])
    ld_ref[...] = (-DECAY_SCALE * _sigmoid(y)).reshape(bb, tt, RW_WIDTH)
    a_ref[...] = _sigmoid(a0_ref[...] + _dot(al, wa2_ref[...])).reshape(bb, tt, RW_WIDTH)
    zb_ref[...] = _dot(hb, wzb_ref[...]).reshape(bb, tt, RW_WIDTH)


def _front_rwkv(x, mod, s0, w, *, bb, tt):
    nb, t_all, _ = x.shape
    grid = (nb // bb, t_all // tt)
    tok = lambda width: pl.BlockSpec((bb, tt, width), lambda i, j: (i, j, 0))
    row = lambda width: pl.BlockSpec((bb, 1, width), lambda i, j: (i, 0, 0))
    widths = (RW_WIDTH, LANES, RW_WIDTH, RW_WIDTH, LANES)
    mus = [w["mu_r"], w["mu_w"], w["mu_k"], w["mu_v"], w["mu_a"]]
    weights = [w["w_r"], w["w_wl"], w["w_k"], w["w_v"], w["w_al"], w["w_zb"], w["w_decay2"], w["w0"], w["w_a2"],
               w["a0"]]
    in_specs = ([tok(D_MODEL), row(3 * D_MODEL), _const_spec((1, 1, D_MODEL))] + [row(n) for n in widths]
                + [_const_spec(a.shape) for a in mus] + [_const_spec(a.shape) for a in weights])
    out_shape = ([jax.ShapeDtypeStruct((nb, t_all, RW_WIDTH), F32)] * 6
                 + [jax.ShapeDtypeStruct((nb, 1, n), F32) for n in widths])
    out_specs = [tok(RW_WIDTH)] * 6 + [row(n) for n in widths]
    scratch = [pltpu.VMEM((bb, 1, n), F32) for n in widths]
    return pl.pallas_call(
        functools.partial(_front_rwkv_kernel, bb=bb, tt=tt),
        grid=grid, in_specs=in_specs, out_specs=out_specs, out_shape=out_shape, scratch_shapes=scratch,
        name="front_rwkv", compiler_params=_params("parallel", "arbitrary"),
    )(x, mod, w["g_norm"], *s0, *mus, *weights)


def _attn_prompt_kernel(q_ref, k_ref, v_ref, za_ref, o_ref, *, tq):
    qi = pl.program_id(1)
    row = lax.broadcasted_iota(jnp.int32, (tq, tq), 0)
    col = lax.broadcasted_iota(jnp.int32, (tq, tq), 1)
    causal = col <= row

    def head(h):
        sl = slice(h * HEAD_PAD, (h + 1) * HEAD_PAD)
        qh = q_ref[0, :, sl]

        def block(j, carry, masked):
            m, l, acc = carry
            rows = pl.ds(pl.multiple_of(j * tq, tq), tq)
            s = _dot_nt(qh, k_ref[0, rows, sl])
            if masked:
                s = jnp.where(causal, s, NEG_BIG)
            m_new = jnp.maximum(m, jnp.max(s, axis=-1, keepdims=True))
            p = jnp.exp(s - m_new)
            alpha = jnp.exp(m - m_new)
            l = alpha * l + jnp.sum(p, axis=-1, keepdims=True)
            acc = alpha * acc + _dot(p, v_ref[0, rows, sl])
            return m_new, l, acc

        init = (jnp.full((tq, 1), NEG_BIG, F32), jnp.zeros((tq, 1), F32), jnp.zeros((tq, HEAD_PAD), F32))
        carry = lax.fori_loop(0, qi, lambda j, c: block(j, c, False), init)
        m, l, acc = block(qi, carry, True)
        return acc / l

    for pair in range(MLA_HEADS // 2):
        o = head(2 * pair) + head(2 * pair + 1)
        sl = slice(pair * LANES, (pair + 1) * LANES)
        o_ref[0, :, sl] = (o * _silu(za_ref[0, :, sl])).astype(BF16)


def _attn_prompt(q, k, v, za, *, tq):
    nb, s_len, _ = q.shape
    return pl.pallas_call(
        functools.partial(_attn_prompt_kernel, tq=tq),
        grid=(nb, s_len // tq),
        in_specs=[pl.BlockSpec((1, tq, QKV_WIDTH), lambda b, i: (b, i, 0)),
                  pl.BlockSpec((1, s_len, QKV_WIDTH), lambda b, i: (b, 0, 0)),
                  pl.BlockSpec((1, s_len, QKV_WIDTH), lambda b, i: (b, 0, 0)),
                  pl.BlockSpec((1, tq, MLA_WIDTH), lambda b, i: (b, i, 0))],
        out_specs=pl.BlockSpec((1, tq, MLA_WIDTH), lambda b, i: (b, i, 0)),
        out_shape=jax.ShapeDtypeStruct((nb, s_len, MLA_WIDTH), BF16),
        name="attn_prompt", compiler_params=_params("parallel", "arbitrary"),
    )(q, k, v, za)


def _attn_sample_kernel(pt_ref, q_ref, kvn_ref, kpn_ref, kin_ref, za_ref, wukt_ref, gkn_ref, wuv_ref, *rest,
                        pp, tt):
    pages = rest[:3 * pp]
    o_ref = rest[3 * pp]
    qlat_ref, qpe_ref, m_ref, l_ref, acc_ref = rest[3 * pp + 1:]
    j = pl.program_id(1)
    rows = MLA_HEADS * tt

    @pl.when(j == 0)
    def _():
        gkn = gkn_ref[...]
        for h in range(MLA_HEADS):
            qh = q_ref[0, :, h * HEAD_PAD:(h + 1) * HEAD_PAD]
            qlat_ref[h * tt:(h + 1) * tt, :] = _dot(qh * gkn, wukt_ref[h])
            qpe_ref[h * tt:(h + 1) * tt, :] = qh[:, :QK_ROPE]
        m_ref[...] = jnp.full((rows, 1), NEG_BIG, F32)
        l_ref[...] = jnp.zeros((rows, 1), F32)
        acc_ref[...] = jnp.zeros((rows, KV_LORA), F32)

    qlat = qlat_ref[...].astype(BF16)
    qpe = qpe_ref[...].astype(BF16)
    expand = (lax.broadcasted_iota(jnp.int32, (rows, MLA_HEADS), 0) // tt
              == lax.broadcasted_iota(jnp.int32, (rows, MLA_HEADS), 1)).astype(BF16)

    def update(c, kpe, kinv, mask):
        cb = c.astype(BF16)
        s = _dot_nt(qlat, cb) + _dot_nt(qpe, kpe)
        k_hi, k_mid, k_lo = _split3(kinv)
        s = s * (_dot_nt(expand, k_hi) + _dot_nt(expand, k_mid) + _dot_nt(expand, k_lo))
        if mask is not None:
            s = jnp.where(mask, s, NEG_BIG)
        m = m_ref[...]
        m_new = jnp.maximum(m, jnp.max(s, axis=-1, keepdims=True))
        p = jnp.exp(s - m_new)
        alpha = jnp.exp(m - m_new)
        l_ref[...] = alpha * l_ref[...] + jnp.sum(p, axis=-1, keepdims=True)
        acc_ref[...] = alpha * acc_ref[...] + _dot(p, cb)
        m_ref[...] = m_new

    for i in range(pp):
        update(pages[3 * i][0], pages[3 * i + 1][0], pages[3 * i + 2][0], None)

    @pl.when(j == pl.num_programs(1) - 1)
    def _():
        pad = lambda x: jnp.concatenate([x, jnp.zeros((PAGE_SIZE - tt, x.shape[1]), F32)], axis=0)
        key = lax.broadcasted_iota(jnp.int32, (rows, PAGE_SIZE), 1)
        tok = lax.broadcasted_iota(jnp.int32, (rows, PAGE_SIZE), 0) % tt
        update(pad(kvn_ref[0]), pad(kpn_ref[0]), pad(kin_ref[0]), key <= tok)
        o_lat = (acc_ref[...] / l_ref[...]).astype(BF16)
        for pair in range(MLA_HEADS // 2):
            o = jnp.zeros((tt, LANES), F32)
            for h in (2 * pair, 2 * pair + 1):
                o = o + _dot(o_lat[h * tt:(h + 1) * tt, :], wuv_ref[:, h * HEAD_PAD:(h + 1) * HEAD_PAD])
            sl = slice(pair * LANES, (pair + 1) * LANES)
            o_ref[0, :, sl] = (o * _silu(za_ref[0, :, sl])).astype(BF16)


def _attn_sample(page_table, q, kv_new, kpe_new, kinv_new, za, cache_kv, cache_kpe, cache_kinv, w, *, pp):
    nb, tt, _ = q.shape
    n_pages = page_table.shape[1]
    rows = MLA_HEADS * tt
    per_b = lambda width: pl.BlockSpec((1, tt, width), lambda b, j, pt: (b, 0, 0))
    const = lambda shape: pl.BlockSpec(shape, lambda b, j, pt: (0,) * len(shape))

    def page(width, i):
        return pl.BlockSpec((1, PAGE_SIZE, width), lambda b, j, pt: (pt[b, j * pp + i], 0, 0))

    page_specs, page_args = [], []
    for i in range(pp):
        page_specs += [page(KV_LORA, i), page(QK_ROPE, i), page(MLA_HEADS, i)]
        page_args += [cache_kv, cache_kpe, cache_kinv]
    grid_spec = pltpu.PrefetchScalarGridSpec(
        num_scalar_prefetch=1, grid=(nb, n_pages // pp),
        in_specs=[per_b(QKV_WIDTH), per_b(KV_LORA), per_b(QK_ROPE), per_b(MLA_HEADS), per_b(MLA_WIDTH),
                  const(w["w_ukt"].shape), const(w["g_k_nope"].shape), const(w["w_uv"].shape)] + page_specs,
        out_specs=per_b(MLA_WIDTH),
        scratch_shapes=[pltpu.VMEM((rows, KV_LORA), F32), pltpu.VMEM((rows, QK_ROPE), F32),
                        pltpu.VMEM((rows, 1), F32), pltpu.VMEM((rows, 1), F32), pltpu.VMEM((rows, KV_LORA), F32)])
    return pl.pallas_call(
        functools.partial(_attn_sample_kernel, pp=pp, tt=tt),
        grid_spec=grid_spec,
        out_shape=jax.ShapeDtypeStruct((nb, tt, MLA_WIDTH), BF16),
        name="attn_sample", compiler_params=_params("parallel", "arbitrary"),
    )(page_table, q, kv_new, kpe_new, kinv_new, za, w["w_ukt"], w["g_k_nope"], w["w_uv"], *page_args)


def _wkv_kernel(*refs, tk, n_chunks, has_state):
    if has_state:
        (r_ref, k_ref, v_ref, ld_ref, a_ref, zb_ref, kk_ref, ka_ref, rk_ref, gg_ref, gb_ref, s0_ref,
         o_ref, so_ref, s_ref) = refs
    else:
        (r_ref, k_ref, v_ref, ld_ref, a_ref, zb_ref, kk_ref, ka_ref, rk_ref, gg_ref, gb_ref,
         o_ref, so_ref, s_ref) = refs
    C, L, G = WKV_CHUNK, GROUP_LANES, WKV_GROUP
    GC = G * C
    ri = lax.broadcasted_iota(jnp.int32, (GC, L), 0)
    ci = lax.broadcasted_iota(jnp.int32, (GC, L), 1)
    same_head = (ri // RW_HEAD) == (ci // RW_HEAD)
    blk = lambda n: (ri // n) == (ci // n)
    m16 = blk(16)
    m32_off = blk(32) & ~m16
    m64_off = blk(64) & ~blk(32)
    eye = (ri == ci).astype(F32)
    t2 = lax.broadcasted_iota(jnp.int32, (C, GC), 0)
    s2 = lax.broadcasted_iota(jnp.int32, (C, GC), 1) % C
    strict2 = s2 < t2
    incl2 = s2 <= t2
    tri = (lax.broadcasted_iota(jnp.int32, (C, C), 1) <= lax.broadcasted_iota(jnp.int32, (C, C), 0)).astype(BF16)
    ones_bd = same_head.astype(BF16)
    replicate = (lax.broadcasted_iota(jnp.int32, (RW_HEAD, L), 1) % RW_HEAD
                 == lax.broadcasted_iota(jnp.int32, (RW_HEAD, L), 0)).astype(BF16)

    def seg_sum(x):
        hi, lo = _split2(x)
        return _dot(hi, ones_bd) + _dot(lo, ones_bd)

    def stack(x):
        return jnp.where(same_head, jnp.concatenate([x] * G, axis=0), 0.0)

    def unit_lower_inverse(l_bd):
        p = -jnp.where(m16, l_bd, 0.0)
        x = eye + p
        for _ in range(3):
            p = _dot(p, p)
            x = x + _dot(x, p)
        for off in (m32_off, m64_off):
            x = x - _dot(x, _dot(jnp.where(off, l_bd, 0.0), x))
        return x

    if has_state:
        hi, mid, lo = _split3(s0_ref[0])
        spread = _dot(hi, replicate) + _dot(mid, replicate) + _dot(lo, replicate)
        s_ref[...] = jnp.where(same_head, spread, 0.0)
    else:
        s_ref[...] = jnp.zeros((L, L), F32)

    k_k, k_a, r_k, gn_g, gn_b = kk_ref[...], ka_ref[...], rk_ref[...], gg_ref[...], gb_ref[...]

    def chunk(c, _):
        if tk == C:
            rows = pl.ds(pl.multiple_of(c * C, C), C)
            load = lambda ref: ref[0, rows, :]
        else:
            rows = slice(0, tk)
            load = lambda ref: jnp.concatenate([ref[0], jnp.zeros((C - tk, L), F32)], axis=0)
        r, k, v, ld, a, zb = (load(ref) for ref in (r_ref, k_ref, v_ref, ld_ref, a_ref, zb_ref))
        ld_hi, ld_lo = _split2(ld)
        cum = _dot(tri, ld_hi) + _dot(tri, ld_lo)
        cum_end = cum[C - 1:C, :]
        kk = k * k_k
        kk = kk * lax.rsqrt(jnp.maximum(seg_sum(kk * kk), 1e-24))
        k = k * (1.0 + (a - 1.0) * k_a)
        b = kk * a
        grow = jnp.exp(-cum)
        tail = jnp.exp(cum_end - cum)
        alpha = kk * jnp.exp(cum - ld)
        rho = r * jnp.exp(cum)
        ar = jnp.concatenate([alpha, rho], axis=0).astype(BF16)
        bk = jnp.concatenate([stack(b * grow), stack(k * grow)], axis=0).astype(BF16)
        g = _dot_nt(ar, bk)
        l2b = jnp.where(strict2, g[:C, :GC], 0.0)
        l2k = jnp.where(strict2, g[:C, GC:], 0.0)
        m2b = jnp.where(incl2, g[C:, :GC], 0.0)
        m2k = jnp.where(incl2, g[C:, GC:], 0.0)
        s_prev = s_ref[...]
        ars = _dot_nt(ar, s_prev)
        vs = stack(v).astype(BF16)
        rhs = ars[:C] + _dot(l2k, vs)
        x = unit_lower_inverse(jnp.where(same_head, jnp.concatenate([l2b] * G, axis=0), 0.0))
        x2 = x[0:C] + x[C:2 * C] + x[2 * C:3 * C] + x[3 * C:4 * C]
        u = _dot(x2, stack(rhs))
        o = ars[C:] + _dot(m2k, vs) - _dot(m2b, stack(u))
        vu_t = jnp.concatenate([v, -u], axis=0).T
        ds = _dot(vu_t, jnp.concatenate([k * tail, b * tail], axis=0))
        s_ref[...] = s_prev * jnp.exp(cum_end) + jnp.where(same_head, ds, 0.0)

        mean = seg_sum(o) * (1.0 / RW_HEAD)
        dev = o - mean
        var = seg_sum(dev * dev) * (1.0 / RW_HEAD)
        o = dev * lax.rsqrt(var + GN_EPS) * gn_g + gn_b
        o = o + seg_sum(r * k * r_k) * v
        o = (o * _silu(zb)).astype(BF16)
        o_ref[0, rows, :] = o[:tk] if tk != C else o
        return 0

    if n_chunks == 1:
        chunk(0, 0)
    else:
        lax.fori_loop(0, n_chunks, chunk, 0)

    rep_t = (lax.broadcasted_iota(jnp.int32, (L, RW_HEAD), 0) % RW_HEAD
             == lax.broadcasted_iota(jnp.int32, (L, RW_HEAD), 1)).astype(BF16)
    hi, mid, lo = _split3(s_ref[...])
    so_ref[0] = _dot(hi, rep_t) + _dot(mid, rep_t) + _dot(lo, rep_t)


def _wkv(r, k, v, ld, a, zb, w, state):
    nb, t_all, _ = r.shape
    has_state = state is not None
    if has_state:
        tk, n_chunks = t_all, 1
    else:
        tk, n_chunks = WKV_CHUNK, t_all // WKV_CHUNK
    n_groups = RW_WIDTH // GROUP_LANES
    seq = pl.BlockSpec((1, t_all, GROUP_LANES), lambda b, g: (b, 0, g))
    vec = pl.BlockSpec((1, GROUP_LANES), lambda b, g: (0, g))
    st = pl.BlockSpec((1, GROUP_LANES, RW_HEAD), lambda b, g: (b, g, 0))
    in_specs = [seq] * 6 + [vec] * 5 + ([st] if has_state else [])
    args = [r, k, v, ld, a, zb, w["k_k"], w["k_a"], w["r_k"], w["gn_g"], w["gn_b"]] + ([state] if has_state else [])
    return pl.pallas_call(
        functools.partial(_wkv_kernel, tk=tk, n_chunks=n_chunks, has_state=has_state),
        grid=(nb, n_groups), in_specs=in_specs, out_specs=[seq, st],
        out_shape=[jax.ShapeDtypeStruct((nb, t_all, RW_WIDTH), BF16),
                   jax.ShapeDtypeStruct((nb, RW_WIDTH, RW_HEAD), F32)],
        scratch_shapes=[pltpu.VMEM((GROUP_LANES, GROUP_LANES), F32)],
        name="wkv", compiler_params=_params("parallel", "parallel"),
    )(*args)


def _merge_kernel(x_ref, mod_ref, oa_ref, ob_ref, ga_ref, gb_ref, wpa_ref, wpb_ref, wout_ref, y_ref, *, bb, tt):
    tm = bb * tt
    flat = lambda ref: ref[...].reshape(tm, ref.shape[2])
    m = (_sigmoid(flat(ga_ref)) * _dot(flat(oa_ref), wpa_ref[...])
         + _sigmoid(flat(gb_ref)) * _dot(flat(ob_ref), wpb_ref[...]))
    gate = mod_ref[...][:, :, 2 * D_MODEL:]
    y_ref[...] = x_ref[...] + gate * _dot(m, wout_ref[...]).reshape(bb, tt, D_MODEL)


def _merge(x, mod, oa, ob, ga, gb, w, *, bb, tt):
    nb, t_all, _ = x.shape
    tok = pl.BlockSpec((bb, tt, D_MODEL), lambda i, j: (i, j, 0))
    return pl.pallas_call(
        functools.partial(_merge_kernel, bb=bb, tt=tt),
        grid=(nb // bb, t_all // tt),
        in_specs=[tok, pl.BlockSpec((bb, 1, 3 * D_MODEL), lambda i, j: (i, 0, 0)), tok, tok, tok, tok,
                  _const_spec(w["w_pa"].shape), _const_spec(w["w_pb"].shape), _const_spec(w["w_out"].shape)],
        out_specs=tok, out_shape=jax.ShapeDtypeStruct(x.shape, F32),
        name="merge", compiler_params=_params("parallel", "parallel"),
    )(x, mod, oa, ob, ga, gb, w["w_pa"], w["w_pb"], w["w_out"])


def _head_slabs(rope_part, nope_part):
    lead = nope_part.shape[:-1]
    parts = []
    if rope_part is None:
        parts.append(jnp.zeros(lead + (NOPE_OFF,), nope_part.dtype))
    else:
        parts += [rope_part, jnp.zeros(lead + (NOPE_OFF - QK_ROPE,), nope_part.dtype)]
    parts.append(nope_part)
    slab = jnp.concatenate(parts, axis=-1)
    return slab.reshape(lead[:-1] + (MLA_HEADS * HEAD_PAD,))


def _pad_cols(x, width):
    return jnp.pad(x, ((0, 0), (0, width - x.shape[1])))


def _prepare_weights(w_ada, b_ada, g_norm, w_in, g_qa, w_uq, g_kva, w_uk, w_uv, g_q, g_k, w_pa, mu_shift, w0,
                     w_decay2, a0, w_a2, k_k, k_a, r_k, gn_g, gn_b, w_pb, w_out):
    bf = lambda x: x.astype(BF16)
    row = lambda x: x.reshape(1, -1)
    splits = (Q_LORA, KV_LORA, QK_ROPE, MLA_WIDTH, RW_WIDTH, DECAY_LORA, RW_WIDTH, RW_WIDTH, A_LORA, RW_WIDTH,
              D_MODEL, D_MODEL)
    offs = [0]
    for s in splits:
        offs.append(offs[-1] + s)
    cols = [w_in[:, offs[i]:offs[i + 1]] for i in range(len(splits))]
    w_qa, w_kva, w_kpe, w_za, w_r, w_wl, w_k, w_v, w_al, w_zb, w_ga, w_gb = cols
    mu = [mu_shift[offs[4] - offs[4] + o0:o1] for o0, o1 in
          ((0, RW_WIDTH), (RW_WIDTH, RW_WIDTH + DECAY_LORA), (RW_WIDTH + DECAY_LORA, 2 * RW_WIDTH + DECAY_LORA),
           (2 * RW_WIDTH + DECAY_LORA, 3 * RW_WIDTH + DECAY_LORA),
           (3 * RW_WIDTH + DECAY_LORA, 3 * RW_WIDTH + DECAY_LORA + A_LORA))]
    pad_row = lambda x: _pad_cols(row(x), LANES).reshape(1, 1, LANES)
    row3 = lambda x: x.reshape(1, 1, -1)
    zeros_h = jnp.zeros((KV_LORA, MLA_HEADS, V_DIM), F32)
    parity = (jnp.arange(MLA_HEADS) % 2 == 0)[None, :, None]
    w_uv_slab = jnp.concatenate([jnp.where(parity, w_uv, zeros_h), jnp.where(parity, zeros_h, w_uv)], axis=-1)
    w_ukt = jnp.pad(jnp.transpose(w_uk, (1, 2, 0)), ((0, 0), (NOPE_OFF, 0), (0, 0)))
    g_k_rope = jnp.pad(g_k[QK_NOPE:], (0, HEAD_PAD - QK_ROPE))
    g_k_nope = jnp.pad(g_k[:QK_NOPE], (NOPE_OFF, 0))
    g_q_slab = jnp.concatenate([g_q[QK_NOPE:], jnp.zeros((NOPE_OFF - QK_ROPE,), F32), g_q[:QK_NOPE]]) * SCALE
    return {
        "w_ada": bf(w_ada), "b_ada": row(b_ada), "g_norm": row3(g_norm),
        "w_qa": bf(w_qa), "w_kva": bf(w_kva), "w_kpe": bf(_pad_cols(w_kpe, HEAD_PAD)), "w_za": bf(w_za),
        "w_ga": bf(w_ga), "w_gb": bf(w_gb), "g_qa": row(g_qa), "g_kva": row(g_kva),
        "w_uq": bf(_head_slabs(w_uq[..., QK_NOPE:], w_uq[..., :QK_NOPE])),
        "w_uk": bf(_head_slabs(None, w_uk)), "w_uv": bf(w_uv_slab.reshape(KV_LORA, QKV_WIDTH)),
        "w_ukt": bf(w_ukt), "g_q": row(g_q_slab), "g_k_rope": row(g_k_rope), "g_k_nope": row(g_k_nope),
        "w_r": bf(w_r), "w_wl": bf(_pad_cols(w_wl, LANES)), "w_k": bf(w_k), "w_v": bf(w_v),
        "w_al": bf(_pad_cols(w_al, LANES)), "w_zb": bf(w_zb),
        "mu_r": row3(mu[0]), "mu_w": pad_row(mu[1]), "mu_k": row3(mu[2]), "mu_v": row3(mu[3]), "mu_a": pad_row(mu[4]),
        "w_decay2": bf(jnp.pad(w_decay2, ((0, LANES - DECAY_LORA), (0, 0)))), "w0": row(w0),
        "w_a2": bf(jnp.pad(w_a2, ((0, LANES - A_LORA), (0, 0)))), "a0": row(a0),
        "k_k": row(k_k), "k_a": row(k_a), "r_k": row(r_k), "gn_g": row(gn_g), "gn_b": row(gn_b),
        "w_pa": bf(w_pa), "w_pb": bf(w_pb), "w_out": bf(w_out),
    }


def _rope_tables(pos):
    inv = ROPE_THETA ** (-jnp.arange(HALF_ROPE, dtype=F32) / HALF_ROPE)
    ang = pos.astype(F32)[:, None] * inv[None, :]
    cos, sin = jnp.cos(ang), jnp.sin(ang)
    n = pos.shape[0]
    rest = jnp.zeros((n, HEAD_PAD - QK_ROPE), F32)
    zero = jnp.zeros((n, HALF_ROPE), F32)
    cos_t = jnp.concatenate([cos, cos, rest + 1.0], axis=1)
    sup_t = jnp.concatenate([zero, sin, rest], axis=1)
    sdn_t = jnp.concatenate([-sin, zero, rest], axis=1)
    return cos_t, sup_t, sdn_t


def _split_shift(state_shift):
    o1 = RW_WIDTH
    o2 = o1 + DECAY_LORA
    o3 = o2 + RW_WIDTH
    o4 = o3 + RW_WIDTH
    n = state_shift.shape[0]
    pad = lambda x: _pad_cols(x, LANES)
    parts = (state_shift[:, :o1], pad(state_shift[:, o1:o2]), state_shift[:, o2:o3], state_shift[:, o3:o4],
             pad(state_shift[:, o4:]))
    return [p.reshape(n, 1, -1) for p in parts]


def _join_shift(last):
    lr, lw, lk, lv, la = (x[:, 0, :] for x in last)
    return jnp.concatenate([lr, lw[:, :DECAY_LORA], lk, lv, la[:, :A_LORA]], axis=1)


def _tile(n, target):
    t = min(n, target)
    while n % t:
        t //= 2
    return t


def kernel(x_prompt, x_sample, c_prompt, c_sample, cache_kv, cache_kpe, cache_kinv, state_wkv, state_shift, page_table, w_ada, b_ada, g_norm, w_in, g_qa, w_uq, g_kva, w_uk, w_uv, g_q, g_k, w_pa, mu_shift, w0, w_decay2, a0, w_a2, k_k, k_a, r_k, gn_g, gn_b, w_pb, w_out):
    B, S, _ = x_prompt.shape
    Bd, T, _ = x_sample.shape
    n_pages = page_table.shape[1]
    past = n_pages * PAGE_SIZE
    w = _prepare_weights(w_ada, b_ada, g_norm, w_in, g_qa, w_uq, g_kva, w_uk, w_uv, g_q, g_k, w_pa, mu_shift, w0,
                         w_decay2, a0, w_a2, k_k, k_a, r_k, gn_g, gn_b, w_pb, w_out)

    mod = _mod(jnp.concatenate([c_prompt, c_sample], axis=0), w["w_ada"], w["b_ada"])
    mod_p = mod[:B].reshape(B, 1, 3 * D_MODEL)
    mod_s = mod[B:].reshape(Bd, 1, 3 * D_MODEL)

    tt = _tile(S, 256)
    q, kv_p, kpe_p, kinv_p, za, ga, gb, keys, vals = _front_mla(
        x_prompt, mod_p, _rope_tables(jnp.arange(S)), w, bb=1, tt=tt, q_dtype=BF16, emit_kv=True)
    zero_shift = _split_shift(jnp.zeros((B, 3 * RW_WIDTH + DECAY_LORA + A_LORA), F32))
    r, k, v, ld, a, zb, *last_p = _front_rwkv(x_prompt, mod_p, zero_shift, w, bb=1, tt=tt)
    o_a = _attn_prompt(q, keys, vals, za, tq=_tile(S, 256))
    o_b, wkv_p = _wkv(r, k, v, ld, a, zb, w, None)
    y_prompt = _merge(x_prompt, mod_p, o_a, o_b, ga, gb, w, bb=1, tt=_tile(S, 512))

    bs = _tile(Bd, 32)
    q_s, kv_s, kpe_s, kinv_s, za_s, ga_s, gb_s = _front_mla(
        x_sample, mod_s, _rope_tables(past + jnp.arange(T)), w, bb=bs, tt=T, q_dtype=F32, emit_kv=False)
    r_s, k_s, v_s, ld_s, a_s, zb_s, *last_s = _front_rwkv(x_sample, mod_s, _split_shift(state_shift), w, bb=bs, tt=T)
    o_a_s = _attn_sample(page_table, q_s, kv_s, kpe_s, kinv_s, za_s, cache_kv, cache_kpe, cache_kinv, w,
                         pp=_tile(n_pages, 8))
    o_b_s, wkv_s = _wkv(r_s, k_s, v_s, ld_s, a_s, zb_s, w, state_wkv.reshape(Bd, RW_WIDTH, RW_HEAD))
    y_sample = _merge(x_sample, mod_s, o_a_s, o_b_s, ga_s, gb_s, w, bb=bs, tt=T)

    return (y_prompt, y_sample, kv_p, kpe_p, kinv_p, wkv_p.reshape(B, RW_HEADS, RW_HEAD, RW_HEAD),
            _join_shift(last_p), kv_s, kpe_s, kinv_s, wkv_s.reshape(Bd, RW_HEADS, RW_HEAD, RW_HEAD),
            _join_shift(last_s))
```

```python
import functools
import math

import jax
import jax.numpy as jnp
from jax import lax
from jax.experimental import pallas as pl
from jax.experimental.pallas import tpu as pltpu

F32 = jnp.float32
BF16 = jnp.bfloat16

D_MODEL = 1024
PAGE_SIZE = 128
MLA_HEADS = 16
QK_NOPE = 64
QK_ROPE = 32
QK_DIM = QK_NOPE + QK_ROPE
V_DIM = 64
Q_LORA = 384
KV_LORA = 256
ROPE_THETA = 10000.0
MLA_WIDTH = MLA_HEADS * V_DIM
SCALE = QK_DIM ** -0.5
RW_HEADS = 16
RW_HEAD = 64
RW_WIDTH = RW_HEADS * RW_HEAD
DECAY_LORA = 64
A_LORA = 64
RMS_EPS = 1e-6
GN_EPS = 64e-5

LANES = 128
HEAD_PAD = LANES
QKV_WIDTH = MLA_HEADS * HEAD_PAD
NOPE_OFF = HEAD_PAD - QK_NOPE
HALF_ROPE = QK_ROPE // 2
WKV_CHUNK = 64
WKV_GROUP = 4
GROUP_LANES = WKV_GROUP * RW_HEAD
VMEM_LIMIT = 56 * 1024 * 1024
NEG_BIG = -1e30
DECAY_SCALE = math.exp(-0.5)
LOG2_E = math.log2(math.e)
HEADS_PER_STAGE = 4


def _dot(a, b):
    return jnp.dot(a.astype(BF16), b.astype(BF16), preferred_element_type=F32)


def _dot_nt(a, b):
    return lax.dot_general(a.astype(BF16), b.astype(BF16), (((1,), (1,)), ((), ())), preferred_element_type=F32)


def _split2(x):
    hi = x.astype(BF16)
    lo = (x - hi.astype(F32)).astype(BF16)
    return hi, lo


def _split3(x):
    hi = x.astype(BF16)
    r1 = x - hi.astype(F32)
    mid = r1.astype(BF16)
    lo = (r1 - mid.astype(F32)).astype(BF16)
    return hi, mid, lo


def _sigmoid(x):
    return 1.0 / (1.0 + jnp.exp(-x))


def _silu(x):
    return x * _sigmoid(x)


def _rms(x, eps=RMS_EPS):
    return x * lax.rsqrt(jnp.mean(x * x, axis=-1, keepdims=True) + eps)


def _params(*sem):
    return pltpu.CompilerParams(dimension_semantics=sem, vmem_limit_bytes=VMEM_LIMIT)


def _const_spec(shape):
    nd = len(shape)
    return pl.BlockSpec(shape, lambda *_: (0,) * nd)


def _mod_kernel(c_ref, w_ref, b_ref, o_ref):
    o_ref[...] = _dot(_silu(c_ref[...]), w_ref[...]) + b_ref[...]


def _mod(c, w_ada, b_ada):
    n = c.shape[0]
    return pl.pallas_call(
        _mod_kernel,
        out_shape=jax.ShapeDtypeStruct((n, 3 * D_MODEL), F32),
        name="mod",
        compiler_params=pltpu.CompilerParams(vmem_limit_bytes=VMEM_LIMIT),
    )(c, w_ada, b_ada)


def _modulated(x_ref, mod_ref, gn_ref, bb, tt):
    x = x_ref[...]
    mod = mod_ref[...]
    shift = mod[:, :, :D_MODEL]
    scale = mod[:, :, D_MODEL:2 * D_MODEL]
    h = _rms(x) * gn_ref[...] * (1.0 + scale) + shift
    return h.reshape(bb * tt, D_MODEL).astype(BF16)


def _rope(x, cos, sup, sdn):
    return x * cos + pltpu.roll(x, HALF_ROPE, 1) * sup + pltpu.roll(x, HEAD_PAD - HALF_ROPE, 1) * sdn


def _front_mla_kernel(x_ref, mod_ref, gn_ref, cos_ref, sup_ref, sdn_ref,
                      wqa_ref, wkva_ref, wkpe_ref, wza_ref, wga_ref, wgb_ref,
                      gqa_ref, wuq_ref, gq_ref, gkva_ref, wuk_ref, gkr_ref, gkn_ref, wuv_ref, vone_ref,
                      *out_refs, bb, tt, emit_kv):
    if emit_kv:
        q_ref, kv_ref, kpe_ref, kinv_ref, za_ref, ga_ref, gb_ref, k_ref, v_ref = out_refs
    else:
        q_ref, kv_ref, kpe_ref, kinv_ref, za_ref, ga_ref, gb_ref = out_refs
    tm = bb * tt
    hb = _modulated(x_ref, mod_ref, gn_ref, bb, tt)

    def table(ref):
        return jnp.broadcast_to(ref[...][None], (bb, tt, HEAD_PAD)).reshape(tm, HEAD_PAD)

    cos, sup, sdn = table(cos_ref), table(sup_ref), table(sdn_ref)

    za_ref[...] = _dot(hb, wza_ref[...]).reshape(bb, tt, MLA_WIDTH)
    ga_ref[...] = _dot(hb, wga_ref[...]).reshape(bb, tt, D_MODEL)
    gb_ref[...] = _dot(hb, wgb_ref[...]).reshape(bb, tt, D_MODEL)

    qn = _rms(_dot(hb, wqa_ref[...])) * gqa_ref[...]
    qf = _dot(qn, wuq_ref[...])
    gq = gq_ref[...]
    for h in range(MLA_HEADS):
        sl = slice(h * HEAD_PAD, (h + 1) * HEAD_PAD)
        qh = qf[:, sl]
        ms = jnp.sum(qh * qh, axis=-1, keepdims=True) * (1.0 / QK_DIM)
        qh = _rope(qh * lax.rsqrt(ms + RMS_EPS) * gq, cos, sup, sdn)
        q_ref[:, :, sl] = qh.reshape(bb, tt, HEAD_PAD).astype(q_ref.dtype)

    c_lat = _rms(_dot(hb, wkva_ref[...])) * gkva_ref[...]
    kv_ref[...] = c_lat.reshape(bb, tt, KV_LORA)
    cb = c_lat.astype(BF16)
    kpe_raw = _dot(hb, wkpe_ref[...])
    kp2 = jnp.sum(kpe_raw * kpe_raw, axis=-1, keepdims=True)
    kpe_rot = _rope(kpe_raw * gkr_ref[...], cos, sup, sdn)
    kpe_ref[...] = kpe_rot[:, :QK_ROPE].reshape(bb, tt, QK_ROPE)

    knf = _dot(cb, wuk_ref[...])
    gkn = gkn_ref[...]
    head_lane = lax.broadcasted_iota(jnp.int32, (tm, MLA_HEADS), 1)
    kinv_all = jnp.zeros((tm, MLA_HEADS), F32)
    for h in range(MLA_HEADS):
        sl = slice(h * HEAD_PAD, (h + 1) * HEAD_PAD)
        kn = knf[:, sl]
        ms = (jnp.sum(kn * kn, axis=-1, keepdims=True) + kp2) * (1.0 / QK_DIM)
        kinv = lax.rsqrt(ms + RMS_EPS)
        kinv_all = jnp.where(head_lane == h, kinv, kinv_all)
        if emit_kv:
            k_ref[:, :, sl] = ((kn * gkn + kpe_rot) * kinv).reshape(bb, tt, HEAD_PAD).astype(BF16)
    kinv_ref[...] = kinv_all.reshape(bb, tt, MLA_HEADS)
    if emit_kv:
        v_ref[...] = (_dot(cb, wuv_ref[...]) + vone_ref[...]).reshape(bb, tt, QKV_WIDTH).astype(BF16)


def _front_mla(x, mod, tabs, w, g_q, *, bb, tt, q_dtype, emit_kv):
    nb, t_all, _ = x.shape
    grid = (nb // bb, t_all // tt)
    tok = lambda width: pl.BlockSpec((bb, tt, width), lambda i, j: (i, j, 0))
    tab = pl.BlockSpec((tt, HEAD_PAD), lambda i, j: (j, 0))
    weights = [w["w_qa"], w["w_kva"], w["w_kpe"], w["w_za"], w["w_ga"], w["w_gb"], w["g_qa"], w["w_uq"], g_q,
               w["g_kva"], w["w_uk"], w["g_k_rope"], w["g_k_nope"], w["w_uv"], w["v_one"]]
    in_specs = ([tok(D_MODEL), pl.BlockSpec((bb, 1, 3 * D_MODEL), lambda i, j: (i, 0, 0)),
                 _const_spec((1, 1, D_MODEL)), tab, tab, tab] + [_const_spec(a.shape) for a in weights])
    sds = lambda width, dt=F32: jax.ShapeDtypeStruct((nb, t_all, width), dt)
    out_shape = [sds(QKV_WIDTH, q_dtype), sds(KV_LORA), sds(QK_ROPE), sds(MLA_HEADS), sds(MLA_WIDTH), sds(D_MODEL),
                 sds(D_MODEL)]
    out_specs = [tok(QKV_WIDTH), tok(KV_LORA), tok(QK_ROPE), tok(MLA_HEADS), tok(MLA_WIDTH), tok(D_MODEL),
                 tok(D_MODEL)]
    if emit_kv:
        out_shape += [sds(QKV_WIDTH, BF16), sds(QKV_WIDTH, BF16)]
        out_specs += [tok(QKV_WIDTH), tok(QKV_WIDTH)]
    return pl.pallas_call(
        functools.partial(_front_mla_kernel, bb=bb, tt=tt, emit_kv=emit_kv),
        grid=grid, in_specs=in_specs, out_specs=out_specs, out_shape=out_shape,
        name="front_mla", compiler_params=_params("parallel", "arbitrary"),
    )(x, mod, w["g_norm"], *tabs, *weights)


def _front_rwkv_kernel(x_ref, mod_ref, gn_ref, s0r_ref, s0w_ref, s0k_ref, s0v_ref, s0a_ref,
                       mur_ref, muw_ref, muk_ref, muv_ref, mua_ref,
                       wr_ref, ww_ref, wk_ref, wv_ref, wa_ref, wzb_ref,
                       wd2_ref, w0_ref, wa2_ref, a0_ref,
                       r_ref, k_ref, v_ref, ld_ref, a_ref, zb_ref,
                       lr_ref, lw_ref, lk_ref, lv_ref, la_ref,
                       cr_ref, cw_ref, ck_ref, cv_ref, ca_ref, *, bb, tt):
    j = pl.program_id(1)
    hb = _modulated(x_ref, mod_ref, gn_ref, bb, tt)

    def shifted(w_ref, s0_ref, mu_ref, last_ref, carry_ref):
        width = w_ref.shape[1]
        p = _dot(hb, w_ref[...]).reshape(bb, tt, width)
        first = jnp.where(j == 0, s0_ref[...], carry_ref[...])
        t_idx = lax.broadcasted_iota(jnp.int32, (bb, tt, width), 1)
        prev = jnp.where(t_idx == 0, first, pltpu.roll(p, 1, 1))
        last = p[:, tt - 1:tt, :]
        last_ref[...] = last
        carry_ref[...] = last
        return (p + (prev - p) * mu_ref[...]).reshape(bb * tt, width)

    r_ref[...] = shifted(wr_ref, s0r_ref, mur_ref, lr_ref, cr_ref).reshape(bb, tt, RW_WIDTH)
    k_ref[...] = shifted(wk_ref, s0k_ref, muk_ref, lk_ref, ck_ref).reshape(bb, tt, RW_WIDTH)
    v_ref[...] = shifted(wv_ref, s0v_ref, muv_ref, lv_ref, cv_ref).reshape(bb, tt, RW_WIDTH)
    wl = shifted(ww_ref, s0w_ref, muw_ref, lw_ref, cw_ref)
    al = shifted(wa_ref, s0a_ref, mua_ref, la_ref, ca_ref)
    y = w0_ref[...] + _dot(jnp.tanh(wl), wd2_ref[...])
    ld_ref[...] = (-DECAY_SCALE * _sigmoid(y)).reshape(bb, tt, RW_WIDTH)
    a_ref[...] = _sigmoid(a0_ref[...] + _dot(al, wa2_ref[...])).reshape(bb, tt, RW_WIDTH)
    zb_ref[...] = _dot(hb, wzb_ref[...]).reshape(bb, tt, RW_WIDTH)


def _front_rwkv(x, mod, s0, w, *, bb, tt):
    nb, t_all, _ = x.shape
    grid = (nb // bb, t_all // tt)
    tok = lambda width: pl.BlockSpec((bb, tt, width), lambda i, j: (i, j, 0))
    row = lambda width: pl.BlockSpec((bb, 1, width), lambda i, j: (i, 0, 0))
    widths = (RW_WIDTH, LANES, RW_WIDTH, RW_WIDTH, LANES)
    mus = [w["mu_r"], w["mu_w"], w["mu_k"], w["mu_v"], w["mu_a"]]
    weights = [w["w_r"], w["w_wl"], w["w_k"], w["w_v"], w["w_al"], w["w_zb"], w["w_decay2"], w["w0"], w["w_a2"],
               w["a0"]]
    in_specs = ([tok(D_MODEL), row(3 * D_MODEL), _const_spec((1, 1, D_MODEL))] + [row(n) for n in widths]
                + [_const_spec(a.shape) for a in mus] + [_const_spec(a.shape) for a in weights])
    out_shape = ([jax.ShapeDtypeStruct((nb, t_all, RW_WIDTH), F32)] * 6
                 + [jax.ShapeDtypeStruct((nb, 1, n), F32) for n in widths])
    out_specs = [tok(RW_WIDTH)] * 6 + [row(n) for n in widths]
    scratch = [pltpu.VMEM((bb, 1, n), F32) for n in widths]
    return pl.pallas_call(
        functools.partial(_front_rwkv_kernel, bb=bb, tt=tt),
        grid=grid, in_specs=in_specs, out_specs=out_specs, out_shape=out_shape, scratch_shapes=scratch,
        name="front_rwkv", compiler_params=_params("parallel", "arbitrary"),
    )(x, mod, w["g_norm"], *s0, *mus, *weights)


def _attn_prompt_kernel(q_ref, k_ref, v_ref, za_ref, o_ref, m_ref, acc_ref, *, tq):
    qi = pl.program_id(1)
    row = lax.broadcasted_iota(jnp.int32, (tq, tq), 0)
    col = lax.broadcasted_iota(jnp.int32, (tq, tq), 1)
    causal = col <= row
    m_ref[...] = jnp.full(m_ref.shape, NEG_BIG, F32)
    acc_ref[...] = jnp.zeros(acc_ref.shape, F32)

    def step(j, masked):
        rows = pl.ds(pl.multiple_of(j * tq, tq), tq)
        for h0 in range(0, MLA_HEADS, HEADS_PER_STAGE):
            heads = list(range(h0, h0 + HEADS_PER_STAGE))
            sls = [slice(h * HEAD_PAD, (h + 1) * HEAD_PAD) for h in heads]
            s = [_dot_nt(q_ref[0, :, sl], k_ref[0, rows, sl]) for sl in sls]
            if masked:
                s = [jnp.where(causal, x, NEG_BIG) for x in s]
            m_old = [m_ref[h] for h in heads]
            m_new = [jnp.maximum(mo, jnp.max(x, axis=-1, keepdims=True)) for mo, x in zip(m_old, s)]
            p = [jnp.exp2(jnp.concatenate([x[:, i:i + LANES] - mn for i in range(0, tq, LANES)], axis=1))
                 for x, mn in zip(s, m_new)]
            pv = [_dot(x, v_ref[0, rows, sl]) for x, sl in zip(p, sls)]
            for h, mo, mn, x in zip(heads, m_old, m_new, pv):
                acc_ref[h] = jnp.exp2(mo - mn) * acc_ref[h] + x
                m_ref[h] = mn

    def full_block(j, carry):
        step(j, False)
        return carry

    lax.fori_loop(0, qi, full_block, 0)
    step(qi, True)

    low_half = lax.broadcasted_iota(jnp.int32, (tq, LANES), 1) < V_DIM
    for pair in range(MLA_HEADS // 2):
        even, odd = acc_ref[2 * pair], acc_ref[2 * pair + 1]
        o = jnp.where(low_half, even / even[:, V_DIM:V_DIM + 1], odd / odd[:, 0:1])
        sl = slice(pair * LANES, (pair + 1) * LANES)
        o_ref[0, :, sl] = (o * _silu(za_ref[0, :, sl])).astype(BF16)


def _attn_prompt(q, k, v, za, *, tq):
    nb, s_len, _ = q.shape
    return pl.pallas_call(
        functools.partial(_attn_prompt_kernel, tq=tq),
        grid=(nb, s_len // tq),
        in_specs=[pl.BlockSpec((1, tq, QKV_WIDTH), lambda b, i: (b, i, 0)),
                  pl.BlockSpec((1, s_len, QKV_WIDTH), lambda b, i: (b, 0, 0)),
                  pl.BlockSpec((1, s_len, QKV_WIDTH), lambda b, i: (b, 0, 0)),
                  pl.BlockSpec((1, tq, MLA_WIDTH), lambda b, i: (b, i, 0))],
        out_specs=pl.BlockSpec((1, tq, MLA_WIDTH), lambda b, i: (b, i, 0)),
        out_shape=jax.ShapeDtypeStruct((nb, s_len, MLA_WIDTH), BF16),
        scratch_shapes=[pltpu.VMEM((MLA_HEADS, tq, LANES), F32), pltpu.VMEM((MLA_HEADS, tq, HEAD_PAD), F32)],
        name="attn_prompt", compiler_params=_params("parallel", "arbitrary"),
    )(q, k, v, za)


def _attn_sample_kernel(pt_ref, q_ref, kvn_ref, kpn_ref, kin_ref, za_ref, wukt_ref, gkn_ref, wuv_ref, *rest,
                        pp, tt):
    pages = rest[:3 * pp]
    o_ref = rest[3 * pp]
    qlat_ref, qpe_ref, m_ref, l_ref, acc_ref = rest[3 * pp + 1:]
    j = pl.program_id(1)
    rows = MLA_HEADS * tt

    @pl.when(j == 0)
    def _():
        gkn = gkn_ref[...]
        for h in range(MLA_HEADS):
            qh = q_ref[0, :, h * HEAD_PAD:(h + 1) * HEAD_PAD]
            qlat_ref[h * tt:(h + 1) * tt, :] = _dot(qh * gkn, wukt_ref[h])
            qpe_ref[h * tt:(h + 1) * tt, :] = qh[:, :QK_ROPE]
        m_ref[...] = jnp.full((rows, 1), NEG_BIG, F32)
        l_ref[...] = jnp.zeros((rows, 1), F32)
        acc_ref[...] = jnp.zeros((rows, KV_LORA), F32)

    qlat = qlat_ref[...].astype(BF16)
    qpe = qpe_ref[...].astype(BF16)

    def update(scores, values):
        m_old = m_ref[...]
        top = scores[0]
        for s in scores[1:]:
            top = jnp.maximum(top, s)
        m_new = jnp.maximum(m_old, jnp.max(top, axis=-1, keepdims=True))
        alpha = jnp.exp(m_old - m_new)
        ps = [jnp.exp(s - m_new) for s in scores]
        tot, pv = ps[0], _dot(ps[0], values[0])
        for p, c in zip(ps[1:], values[1:]):
            tot = tot + p
            pv = pv + _dot(p, c)
        l_ref[...] = alpha * l_ref[...] + jnp.sum(tot, axis=-1, keepdims=True)
        acc_ref[...] = alpha * acc_ref[...] + pv
        m_ref[...] = m_new

    values = [pages[3 * i][0].astype(BF16) for i in range(pp)]
    lat = [_dot_nt(qlat, cb) for cb in values]
    pe = [_dot(qpe, pages[3 * i + 1][0]) for i in range(pp)]
    scores = [((x + y).reshape(MLA_HEADS, tt, PAGE_SIZE) * pages[3 * i + 2][0][:, None, :]).reshape(rows, PAGE_SIZE)
              for i, (x, y) in enumerate(zip(lat, pe))]
    update(scores, values)

    @pl.when(j == pl.num_programs(1) - 1)
    def _():
        pad = lambda x: jnp.concatenate([x, jnp.zeros((PAGE_SIZE - tt, x.shape[1]), F32)], axis=0)
        key = lax.broadcasted_iota(jnp.int32, (rows, PAGE_SIZE), 1)
        tok = lax.broadcasted_iota(jnp.int32, (rows, PAGE_SIZE), 0) % tt
        expand = (lax.broadcasted_iota(jnp.int32, (rows, MLA_HEADS), 0) // tt
                  == lax.broadcasted_iota(jnp.int32, (rows, MLA_HEADS), 1)).astype(BF16)
        cn = pad(kvn_ref[0]).astype(BF16)
        k_hi, k_mid, k_lo = _split3(pad(kin_ref[0]))
        s = _dot_nt(qlat, cn) + _dot_nt(qpe, pad(kpn_ref[0]))
        s = s * (_dot_nt(expand, k_hi) + _dot_nt(expand, k_mid) + _dot_nt(expand, k_lo))
        update([jnp.where(key <= tok, s, NEG_BIG)], [cn])
        o_lat = (acc_ref[...] / l_ref[...]).astype(BF16)
        for pair in range(MLA_HEADS // 2):
            o = jnp.zeros((tt, LANES), F32)
            for h in (2 * pair, 2 * pair + 1):
                o = o + _dot(o_lat[h * tt:(h + 1) * tt, :], wuv_ref[:, h * HEAD_PAD:(h + 1) * HEAD_PAD])
            sl = slice(pair * LANES, (pair + 1) * LANES)
            o_ref[0, :, sl] = (o * _silu(za_ref[0, :, sl])).astype(BF16)


def _attn_sample(page_table, q, kv_new, kpe_new, kinv_new, za, cache_kv, cache_kpe, cache_kinv, w, *, pp):
    nb, tt, _ = q.shape
    n_pages = page_table.shape[1]
    rows = MLA_HEADS * tt
    per_b = lambda width: pl.BlockSpec((1, tt, width), lambda b, j, pt: (b, 0, 0))
    const = lambda shape: pl.BlockSpec(shape, lambda b, j, pt: (0,) * len(shape))

    def page(shape, i):
        return pl.BlockSpec((1,) + shape, lambda b, j, pt: (pt[b, j * pp + i], 0, 0))

    kpe_t = jnp.swapaxes(cache_kpe, 1, 2)
    kinv_t = jnp.swapaxes(cache_kinv, 1, 2)
    page_specs, page_args = [], []
    for i in range(pp):
        page_specs += [page((PAGE_SIZE, KV_LORA), i), page((QK_ROPE, PAGE_SIZE), i), page((MLA_HEADS, PAGE_SIZE), i)]
        page_args += [cache_kv, kpe_t, kinv_t]
    grid_spec = pltpu.PrefetchScalarGridSpec(
        num_scalar_prefetch=1, grid=(nb, n_pages // pp),
        in_specs=[per_b(QKV_WIDTH), per_b(KV_LORA), per_b(QK_ROPE), per_b(MLA_HEADS), per_b(MLA_WIDTH),
                  const(w["w_ukt"].shape), const(w["g_k_nope"].shape), const(w["w_uv"].shape)] + page_specs,
        out_specs=per_b(MLA_WIDTH),
        scratch_shapes=[pltpu.VMEM((rows, KV_LORA), F32), pltpu.VMEM((rows, QK_ROPE), F32),
                        pltpu.VMEM((rows, 1), F32), pltpu.VMEM((rows, 1), F32), pltpu.VMEM((rows, KV_LORA), F32)])
    return pl.pallas_call(
        functools.partial(_attn_sample_kernel, pp=pp, tt=tt),
        grid_spec=grid_spec,
        out_shape=jax.ShapeDtypeStruct((nb, tt, MLA_WIDTH), BF16),
        name="attn_sample", compiler_params=_params("parallel", "arbitrary"),
    )(page_table, q, kv_new, kpe_new, kinv_new, za, w["w_ukt"], w["g_k_nope"], w["w_uv"], *page_args)


def _wkv_kernel(*refs, tk, n_chunks, has_state):
    if has_state:
        (r_ref, k_ref, v_ref, ld_ref, a_ref, zb_ref, kk_ref, ka_ref, rk_ref, gg_ref, gb_ref, s0_ref,
         o_ref, so_ref, s_ref) = refs
    else:
        (r_ref, k_ref, v_ref, ld_ref, a_ref, zb_ref, kk_ref, ka_ref, rk_ref, gg_ref, gb_ref,
         o_ref, so_ref, s_ref) = refs
    C, L, G = WKV_CHUNK, GROUP_LANES, WKV_GROUP
    GC = G * C
    n_groups = RW_WIDTH // L
    ti = pl.program_id(1)
    ri = lax.broadcasted_iota(jnp.int32, (GC, L), 0)
    ci = lax.broadcasted_iota(jnp.int32, (GC, L), 1)
    blk = lambda n: (ri // n) == (ci // n)
    same_head = blk(RW_HEAD).astype(F32)
    neg_m16 = -blk(16).astype(BF16)
    m32_off = (blk(32) & ~blk(16)).astype(BF16)
    m64_off = (blk(64) & ~blk(32)).astype(BF16)
    eye = (ri == ci).astype(F32)
    t2 = lax.broadcasted_iota(jnp.int32, (C, GC), 0)
    s2 = lax.broadcasted_iota(jnp.int32, (C, GC), 1) % C
    strict2 = (s2 < t2).astype(F32)
    incl2 = (s2 <= t2).astype(F32)
    tri = (lax.broadcasted_iota(jnp.int32, (C, C), 1) <= lax.broadcasted_iota(jnp.int32, (C, C), 0)).astype(BF16)
    ones_bd = same_head.astype(BF16)
    replicate = (lax.broadcasted_iota(jnp.int32, (RW_HEAD, L), 1) % RW_HEAD
                 == lax.broadcasted_iota(jnp.int32, (RW_HEAD, L), 0)).astype(BF16)

    def seg_sum(x):
        hi, lo = _split2(x)
        return _dot(hi, ones_bd) + _dot(lo, ones_bd)

    def stack(x):
        return jnp.concatenate([x] * G, axis=0) * same_head

    @pl.when(ti == 0)
    def _():
        if has_state:
            for g in range(n_groups):
                hi, mid, lo = _split3(s0_ref[0, g * L:(g + 1) * L, :])
                s_ref[g] = (_dot(hi, replicate) + _dot(mid, replicate) + _dot(lo, replicate)) * same_head
        else:
            s_ref[...] = jnp.zeros(s_ref.shape, F32)

    def each(f, *cols):
        return [f(*args) for args in zip(*cols)]

    def chunk(c, carry):
        groups = list(range(n_groups))
        lanes = [slice(g * L, (g + 1) * L) for g in groups]
        if tk == C:
            rows = pl.ds(pl.multiple_of(c * C, C), C)
            load = lambda ref: [ref[0, rows, ln] for ln in lanes]
        else:
            rows = slice(0, tk)
            load = lambda ref: [jnp.concatenate([ref[0, :, ln], jnp.zeros((C - tk, L), F32)], axis=0) for ln in lanes]
        r, k, v, ld, a, zb = (load(ref) for ref in (r_ref, k_ref, v_ref, ld_ref, a_ref, zb_ref))
        k_k, k_a, r_k, gn_g, gn_b = ([ref[:, ln] for ln in lanes] for ref in (kk_ref, ka_ref, rk_ref, gg_ref, gb_ref))
        ld_split = each(_split2, ld)
        cum = each(lambda hl: _dot(tri, hl[0]) + _dot(tri, hl[1]), ld_split)
        cum_end = each(lambda x: x[C - 1:C, :], cum)
        kk = each(lambda x, y: x * y, k, k_k)
        norm = each(lambda x: seg_sum(x * x), kk)
        kk = each(lambda x, n: x * lax.rsqrt(jnp.maximum(n, 1e-24)), kk, norm)
        k = each(lambda x, a_, ka: x * (1.0 + (a_ - 1.0) * ka), k, a, k_a)
        b = each(lambda x, y: x * y, kk, a)
        grow = each(lambda x: jnp.exp(-x), cum)
        tail = each(lambda e, x: jnp.exp(e - x), cum_end, cum)
        ar = each(lambda kk_, x, l_, r_: jnp.concatenate([kk_ * jnp.exp(x - l_), r_ * jnp.exp(x)], axis=0).astype(BF16),
                  kk, cum, ld, r)
        bk = each(lambda b_, k_, g_: jnp.concatenate([stack(b_ * g_), stack(k_ * g_)], axis=0).astype(BF16),
                  b, k, grow)
        gm = each(_dot_nt, ar, bk)
        s_prev = [s_ref[g] for g in groups]
        ars = each(_dot_nt, ar, s_prev)
        vs = each(lambda x: stack(x).astype(BF16), v)
        l_rep = each(lambda m: jnp.concatenate([(m[:C, :GC] * strict2).astype(BF16)] * G, axis=0), gm)
        rhs = each(lambda s_, m, vs_: s_[:C] + _dot(m[:C, GC:] * strict2, vs_), ars, gm, vs)
        p = each(lambda x: x * neg_m16, l_rep)
        x = each(lambda p_: eye + p_.astype(F32), p)
        for _ in range(3):
            p = each(lambda p_: _dot(p_, p_).astype(BF16), p)
            x = each(lambda x_, p_: x_ + _dot(x_, p_), x, p)
        for off in (m32_off, m64_off):
            y = each(lambda l_, x_: _dot(l_ * off, x_), l_rep, x)
            x = each(lambda x_, y_: x_ - _dot(x_, y_), x, y)
        x2 = each(lambda x_: x_[0:C] + x_[C:2 * C] + x_[2 * C:3 * C] + x_[3 * C:4 * C], x)
        u = each(lambda x_, rhs_: _dot(x_, stack(rhs_)), x2, rhs)
        o = each(lambda s_, m, vs_, u_: s_[C:] + _dot(m[C:, GC:] * incl2, vs_) - _dot(m[C:, :GC] * incl2, stack(u_)),
                 ars, gm, vs, u)
        ds = each(lambda v_, u_, k_, b_, t_: _dot(jnp.concatenate([v_, -u_], axis=0).T,
                                                   jnp.concatenate([k_ * t_, b_ * t_], axis=0)),
                  v, u, k, b, tail)
        for g in groups:
            s_ref[g] = s_prev[g] * jnp.exp(cum_end[g]) + ds[g] * same_head

        mean = each(lambda x: seg_sum(x) * (1.0 / RW_HEAD), o)
        dev = each(lambda x, m: x - m, o, mean)
        var = each(lambda x: seg_sum(x * x) * (1.0 / RW_HEAD), dev)
        bonus = each(lambda r_, k_, rk: seg_sum(r_ * k_ * rk), r, k, r_k)
        for g in groups:
            out = dev[g] * lax.rsqrt(var[g] + GN_EPS) * gn_g[g] + gn_b[g] + bonus[g] * v[g]
            out = (out * _silu(zb[g])).astype(BF16)
            o_ref[0, rows, lanes[g]] = out[:tk] if tk != C else out
        return carry

    if n_chunks == 1:
        chunk(0, 0)
    else:
        lax.fori_loop(0, n_chunks, chunk, 0)

    @pl.when(ti == pl.num_programs(1) - 1)
    def _():
        rep_t = (lax.broadcasted_iota(jnp.int32, (L, RW_HEAD), 0) % RW_HEAD
                 == lax.broadcasted_iota(jnp.int32, (L, RW_HEAD), 1)).astype(BF16)
        for g in range(n_groups):
            hi, mid, lo = _split3(s_ref[g])
            so_ref[0, g * L:(g + 1) * L, :] = _dot(hi, rep_t) + _dot(mid, rep_t) + _dot(lo, rep_t)


def _wkv(r, k, v, ld, a, zb, w, state):
    nb, t_all, _ = r.shape
    has_state = state is not None
    if has_state:
        tk, tb = t_all, t_all
    else:
        tk, tb = WKV_CHUNK, _tile(t_all, 512)
    n_groups = RW_WIDTH // GROUP_LANES
    seq = pl.BlockSpec((1, tb, RW_WIDTH), lambda b, t: (b, t, 0))
    vec = pl.BlockSpec((1, RW_WIDTH), lambda b, t: (0, 0))
    st = pl.BlockSpec((1, RW_WIDTH, RW_HEAD), lambda b, t: (b, 0, 0))
    in_specs = [seq] * 6 + [vec] * 5 + ([st] if has_state else [])
    args = [r, k, v, ld, a, zb, w["k_k"], w["k_a"], w["r_k"], w["gn_g"], w["gn_b"]] + ([state] if has_state else [])
    return pl.pallas_call(
        functools.partial(_wkv_kernel, tk=tk, n_chunks=tb // tk, has_state=has_state),
        grid=(nb, t_all // tb), in_specs=in_specs, out_specs=[seq, st],
        out_shape=[jax.ShapeDtypeStruct((nb, t_all, RW_WIDTH), BF16),
                   jax.ShapeDtypeStruct((nb, RW_WIDTH, RW_HEAD), F32)],
        scratch_shapes=[pltpu.VMEM((n_groups, GROUP_LANES, GROUP_LANES), F32)],
        name="wkv", compiler_params=_params("parallel", "arbitrary"),
    )(*args)


def _merge_kernel(x_ref, mod_ref, oa_ref, ob_ref, ga_ref, gb_ref, wpa_ref, wpb_ref, wout_ref, y_ref, *, bb, tt):
    tm = bb * tt
    flat = lambda ref: ref[...].reshape(tm, ref.shape[2])
    m = (_sigmoid(flat(ga_ref)) * _dot(flat(oa_ref), wpa_ref[...])
         + _sigmoid(flat(gb_ref)) * _dot(flat(ob_ref), wpb_ref[...]))
    gate = mod_ref[...][:, :, 2 * D_MODEL:]
    y_ref[...] = x_ref[...] + gate * _dot(m, wout_ref[...]).reshape(bb, tt, D_MODEL)


def _merge(x, mod, oa, ob, ga, gb, w, *, bb, tt):
    nb, t_all, _ = x.shape
    tok = pl.BlockSpec((bb, tt, D_MODEL), lambda i, j: (i, j, 0))
    return pl.pallas_call(
        functools.partial(_merge_kernel, bb=bb, tt=tt),
        grid=(nb // bb, t_all // tt),
        in_specs=[tok, pl.BlockSpec((bb, 1, 3 * D_MODEL), lambda i, j: (i, 0, 0)), tok, tok, tok, tok,
                  _const_spec(w["w_pa"].shape), _const_spec(w["w_pb"].shape), _const_spec(w["w_out"].shape)],
        out_specs=tok, out_shape=jax.ShapeDtypeStruct(x.shape, F32),
        name="merge", compiler_params=_params("parallel", "parallel"),
    )(x, mod, oa, ob, ga, gb, w["w_pa"], w["w_pb"], w["w_out"])


def _head_slabs(rope_part, nope_part):
    lead = nope_part.shape[:-1]
    parts = []
    if rope_part is None:
        parts.append(jnp.zeros(lead + (NOPE_OFF,), nope_part.dtype))
    else:
        parts += [rope_part, jnp.zeros(lead + (NOPE_OFF - QK_ROPE,), nope_part.dtype)]
    parts.append(nope_part)
    slab = jnp.concatenate(parts, axis=-1)
    return slab.reshape(lead[:-1] + (MLA_HEADS * HEAD_PAD,))


def _pad_cols(x, width):
    return jnp.pad(x, ((0, 0), (0, width - x.shape[1])))


def _prepare_weights(w_ada, b_ada, g_norm, w_in, g_qa, w_uq, g_kva, w_uk, w_uv, g_q, g_k, w_pa, mu_shift, w0,
                     w_decay2, a0, w_a2, k_k, k_a, r_k, gn_g, gn_b, w_pb, w_out):
    bf = lambda x: x.astype(BF16)
    row = lambda x: x.reshape(1, -1)
    splits = (Q_LORA, KV_LORA, QK_ROPE, MLA_WIDTH, RW_WIDTH, DECAY_LORA, RW_WIDTH, RW_WIDTH, A_LORA, RW_WIDTH,
              D_MODEL, D_MODEL)
    offs = [0]
    for s in splits:
        offs.append(offs[-1] + s)
    cols = [w_in[:, offs[i]:offs[i + 1]] for i in range(len(splits))]
    w_qa, w_kva, w_kpe, w_za, w_r, w_wl, w_k, w_v, w_al, w_zb, w_ga, w_gb = cols
    mu = [mu_shift[offs[i] - offs[4]:offs[i + 1] - offs[4]] for i in range(4, 9)]
    pad_row = lambda x: _pad_cols(row(x), LANES).reshape(1, 1, LANES)
    row3 = lambda x: x.reshape(1, 1, -1)
    zeros_h = jnp.zeros((KV_LORA, MLA_HEADS, V_DIM), F32)
    parity = (jnp.arange(MLA_HEADS) % 2 == 0)[None, :, None]
    w_uv_slab = jnp.concatenate([jnp.where(parity, w_uv, zeros_h), jnp.where(parity, zeros_h, w_uv)], axis=-1)
    w_ukt = jnp.pad(jnp.transpose(w_uk, (1, 2, 0)), ((0, 0), (NOPE_OFF, 0), (0, 0)))
    g_k_rope = jnp.pad(g_k[QK_NOPE:], (0, HEAD_PAD - QK_ROPE))
    g_k_nope = jnp.pad(g_k[:QK_NOPE], (NOPE_OFF, 0))
    g_q_slab = jnp.concatenate([g_q[QK_NOPE:], jnp.zeros((NOPE_OFF - QK_ROPE,), F32), g_q[:QK_NOPE]]) * SCALE
    lane = jnp.arange(QKV_WIDTH) % HEAD_PAD
    even = (jnp.arange(QKV_WIDTH) // HEAD_PAD) % 2 == 0
    v_one = jnp.where(even, lane == V_DIM, lane == 0).astype(F32)
    return {
        "w_ada": bf(w_ada), "b_ada": row(b_ada), "g_norm": row3(g_norm),
        "w_qa": bf(w_qa), "w_kva": bf(w_kva), "w_kpe": bf(_pad_cols(w_kpe, HEAD_PAD)), "w_za": bf(w_za),
        "w_ga": bf(w_ga), "w_gb": bf(w_gb), "g_qa": row(g_qa), "g_kva": row(g_kva),
        "w_uq": bf(_head_slabs(w_uq[..., QK_NOPE:], w_uq[..., :QK_NOPE])),
        "w_uk": bf(_head_slabs(None, w_uk)), "w_uv": bf(w_uv_slab.reshape(KV_LORA, QKV_WIDTH)),
        "w_ukt": bf(w_ukt), "g_q": row(g_q_slab), "g_q_log2": row(g_q_slab * LOG2_E), "v_one": row(v_one),
        "g_k_rope": row(g_k_rope), "g_k_nope": row(g_k_nope),
        "w_r": bf(w_r), "w_wl": bf(_pad_cols(w_wl, LANES)), "w_k": bf(w_k), "w_v": bf(w_v),
        "w_al": bf(_pad_cols(w_al, LANES)), "w_zb": bf(w_zb),
        "mu_r": row3(mu[0]), "mu_w": pad_row(mu[1]), "mu_k": row3(mu[2]), "mu_v": row3(mu[3]), "mu_a": pad_row(mu[4]),
        "w_decay2": bf(jnp.pad(w_decay2, ((0, LANES - DECAY_LORA), (0, 0)))), "w0": row(w0),
        "w_a2": bf(jnp.pad(w_a2, ((0, LANES - A_LORA), (0, 0)))), "a0": row(a0),
        "k_k": row(k_k), "k_a": row(k_a), "r_k": row(r_k), "gn_g": row(gn_g), "gn_b": row(gn_b),
        "w_pa": bf(w_pa), "w_pb": bf(w_pb), "w_out": bf(w_out),
    }


def _rope_tables(pos):
    inv = ROPE_THETA ** (-jnp.arange(HALF_ROPE, dtype=F32) / HALF_ROPE)
    ang = pos.astype(F32)[:, None] * inv[None, :]
    cos, sin = jnp.cos(ang), jnp.sin(ang)
    n = pos.shape[0]
    rest = jnp.zeros((n, HEAD_PAD - QK_ROPE), F32)
    zero = jnp.zeros((n, HALF_ROPE), F32)
    cos_t = jnp.concatenate([cos, cos, rest + 1.0], axis=1)
    sup_t = jnp.concatenate([zero, sin, rest], axis=1)
    sdn_t = jnp.concatenate([-sin, zero, rest], axis=1)
    return cos_t, sup_t, sdn_t


def _split_shift(state_shift):
    o1 = RW_WIDTH
    o2 = o1 + DECAY_LORA
    o3 = o2 + RW_WIDTH
    o4 = o3 + RW_WIDTH
    n = state_shift.shape[0]
    pad = lambda x: _pad_cols(x, LANES)
    parts = (state_shift[:, :o1], pad(state_shift[:, o1:o2]), state_shift[:, o2:o3], state_shift[:, o3:o4],
             pad(state_shift[:, o4:]))
    return [p.reshape(n, 1, -1) for p in parts]


def _join_shift(last):
    lr, lw, lk, lv, la = (x[:, 0, :] for x in last)
    return jnp.concatenate([lr, lw[:, :DECAY_LORA], lk, lv, la[:, :A_LORA]], axis=1)


def _tile(n, target):
    t = min(n, target)
    while n % t:
        t //= 2
    return t


def kernel(x_prompt, x_sample, c_prompt, c_sample, cache_kv, cache_kpe, cache_kinv, state_wkv, state_shift, page_table, w_ada, b_ada, g_norm, w_in, g_qa, w_uq, g_kva, w_uk, w_uv, g_q, g_k, w_pa, mu_shift, w0, w_decay2, a0, w_a2, k_k, k_a, r_k, gn_g, gn_b, w_pb, w_out):
    B, S, _ = x_prompt.shape
    Bd, T, _ = x_sample.shape
    n_pages = page_table.shape[1]
    past = n_pages * PAGE_SIZE
    w = _prepare_weights(w_ada, b_ada, g_norm, w_in, g_qa, w_uq, g_kva, w_uk, w_uv, g_q, g_k, w_pa, mu_shift, w0,
                         w_decay2, a0, w_a2, k_k, k_a, r_k, gn_g, gn_b, w_pb, w_out)

    mod = _mod(jnp.concatenate([c_prompt, c_sample], axis=0), w["w_ada"], w["b_ada"])
    mod_p = mod[:B].reshape(B, 1, 3 * D_MODEL)
    mod_s = mod[B:].reshape(Bd, 1, 3 * D_MODEL)

    tt = _tile(S, 256)
    q, kv_p, kpe_p, kinv_p, za, ga, gb, keys, vals = _front_mla(
        x_prompt, mod_p, _rope_tables(jnp.arange(S)), w, w["g_q_log2"], bb=1, tt=tt, q_dtype=BF16, emit_kv=True)
    zero_shift = _split_shift(jnp.zeros((B, 3 * RW_WIDTH + DECAY_LORA + A_LORA), F32))
    r, k, v, ld, a, zb, *last_p = _front_rwkv(x_prompt, mod_p, zero_shift, w, bb=1, tt=tt)
    o_a = _attn_prompt(q, keys, vals, za, tq=_tile(S, 256))
    o_b, wkv_p = _wkv(r, k, v, ld, a, zb, w, None)
    y_prompt = _merge(x_prompt, mod_p, o_a, o_b, ga, gb, w, bb=1, tt=_tile(S, 512))

    bs = _tile(Bd, 32)
    q_s, kv_s, kpe_s, kinv_s, za_s, ga_s, gb_s = _front_mla(
        x_sample, mod_s, _rope_tables(past + jnp.arange(T)), w, w["g_q"], bb=bs, tt=T, q_dtype=F32, emit_kv=False)
    r_s, k_s, v_s, ld_s, a_s, zb_s, *last_s = _front_rwkv(x_sample, mod_s, _split_shift(state_shift), w, bb=bs, tt=T)
    o_a_s = _attn_sample(page_table, q_s, kv_s, kpe_s, kinv_s, za_s, cache_kv, cache_kpe, cache_kinv, w,
                         pp=_tile(n_pages, 16))
    o_b_s, wkv_s = _wkv(r_s, k_s, v_s, ld_s, a_s, zb_s, w, state_wkv.reshape(Bd, RW_WIDTH, RW_HEAD))
    y_sample = _merge(x_sample, mod_s, o_a_s, o_b_s, ga_s, gb_s, w, bb=bs, tt=T)

    return (y_prompt, y_sample, kv_p, kpe_p, kinv_p, wkv_p.reshape(B, RW_HEADS, RW_HEAD, RW_HEAD),
            _join_shift(last_p), kv_s, kpe_s, kinv_s, wkv_s.reshape(Bd, RW_HEADS, RW_HEAD, RW_HEAD),
            _join_shift(last_s))
```

```python
import functools
import math

import jax
import jax.numpy as jnp
from jax import lax
from jax.experimental import pallas as pl
from jax.experimental.pallas import tpu as pltpu

F32 = jnp.float32
BF16 = jnp.bfloat16

D_MODEL = 1024
PAGE_SIZE = 128
MLA_HEADS = 16
QK_NOPE = 64
QK_ROPE = 32
QK_DIM = QK_NOPE + QK_ROPE
V_DIM = 64
Q_LORA = 384
KV_LORA = 256
ROPE_THETA = 10000.0
MLA_WIDTH = MLA_HEADS * V_DIM
SCALE = QK_DIM ** -0.5
RW_HEADS = 16
RW_HEAD = 64
RW_WIDTH = RW_HEADS * RW_HEAD
DECAY_LORA = 64
A_LORA = 64
RMS_EPS = 1e-6
GN_EPS = 64e-5

LANES = 128
HEAD_PAD = LANES
QKV_WIDTH = MLA_HEADS * HEAD_PAD
NOPE_OFF = HEAD_PAD - QK_NOPE
HALF_ROPE = QK_ROPE // 2
WKV_CHUNK = 64
WKV_GROUP = 4
GROUP_LANES = WKV_GROUP * RW_HEAD
VMEM_LIMIT = 56 * 1024 * 1024
NEG_BIG = -1e30
DECAY_SCALE = math.exp(-0.5)
LOG2_E = math.log2(math.e)
HEADS_PER_STAGE = 8
SOFTMAX_STEPS = 2
GATE_SLICE = 256


def _dot(a, b):
    return jnp.dot(a.astype(BF16), b.astype(BF16), preferred_element_type=F32)


def _dot_nt(a, b):
    return lax.dot_general(a.astype(BF16), b.astype(BF16), (((1,), (1,)), ((), ())), preferred_element_type=F32)


def _split2(x):
    hi = x.astype(BF16)
    lo = (x - hi.astype(F32)).astype(BF16)
    return hi, lo


def _split3(x):
    hi = x.astype(BF16)
    r1 = x - hi.astype(F32)
    mid = r1.astype(BF16)
    lo = (r1 - mid.astype(F32)).astype(BF16)
    return hi, mid, lo


def _sigmoid(x):
    return 1.0 / (1.0 + jnp.exp(-x))


def _silu(x):
    return x * _sigmoid(x)


def _rms(x, eps=RMS_EPS):
    return x * lax.rsqrt(jnp.mean(x * x, axis=-1, keepdims=True) + eps)


def _params(*sem):
    return pltpu.CompilerParams(dimension_semantics=sem, vmem_limit_bytes=VMEM_LIMIT)


def _const_spec(shape):
    nd = len(shape)
    return pl.BlockSpec(shape, lambda *_: (0,) * nd)


def _mod_kernel(c_ref, w_ref, b_ref, o_ref):
    o_ref[...] = _dot(_silu(c_ref[...]), w_ref[...]) + b_ref[...]


def _mod(c, w_ada, b_ada):
    n = c.shape[0]
    return pl.pallas_call(
        _mod_kernel,
        out_shape=jax.ShapeDtypeStruct((n, 3 * D_MODEL), F32),
        name="mod",
        compiler_params=pltpu.CompilerParams(vmem_limit_bytes=VMEM_LIMIT),
    )(c, w_ada, b_ada)


def _modulated(x_ref, mod_ref, gn_ref, bb, tt):
    x = x_ref[...]
    mod = mod_ref[...]
    shift = mod[:, :, :D_MODEL]
    scale = mod[:, :, D_MODEL:2 * D_MODEL]
    h = _rms(x) * gn_ref[...] * (1.0 + scale) + shift
    return h.reshape(bb * tt, D_MODEL).astype(BF16)


def _front_mla_kernel(x_ref, mod_ref, gn_ref, cos_ref, sin_ref,
                      wqa_ref, wkva_ref, wkpe_ref, wza_ref, wga_ref, wgb_ref,
                      gqa_ref, wuq_ref, wuqs_ref, gq_ref, gqs_ref, gkva_ref, wuk_ref, gkr_ref, gkrs_ref, gkn_ref,
                      wuv_ref, vone_ref, *out_refs, bb, tt, emit_kv):
    if emit_kv:
        q_ref, kv_ref, kpe_ref, kinv_ref, za_ref, ga_ref, gb_ref, k_ref, v_ref = out_refs
    else:
        q_ref, kv_ref, kpe_ref, kinv_ref, za_ref, ga_ref, gb_ref = out_refs
    tm = bb * tt
    hb = _modulated(x_ref, mod_ref, gn_ref, bb, tt)

    def table(ref):
        return jnp.broadcast_to(ref[...][None], (bb, tt, HEAD_PAD)).reshape(tm, HEAD_PAD)

    cos, sin = table(cos_ref), table(sin_ref)

    gate_slices = [(o_ref, w_ref, c) for o_ref, w_ref in ((za_ref, wza_ref), (ga_ref, wga_ref), (gb_ref, wgb_ref))
                   for c in range(0, D_MODEL, GATE_SLICE)]
    n_stages = 2 * MLA_HEADS

    def gates_due(stage):
        lo = stage * len(gate_slices) // n_stages
        hi = (stage + 1) * len(gate_slices) // n_stages
        for o_ref, w_ref, c in gate_slices[lo:hi]:
            o_ref[:, :, c:c + GATE_SLICE] = _dot(hb, w_ref[:, c:c + GATE_SLICE]).reshape(bb, tt, GATE_SLICE)

    qn = (_rms(_dot(hb, wqa_ref[...])) * gqa_ref[...]).astype(BF16)
    qf = _dot(qn, wuq_ref[...])
    qs = _dot(qn, wuqs_ref[...])
    c_lat = _rms(_dot(hb, wkva_ref[...])) * gkva_ref[...]
    kv_ref[...] = c_lat.reshape(bb, tt, KV_LORA)
    cb = c_lat.astype(BF16)
    kpe2 = _dot(hb, wkpe_ref[...])
    kpe_raw = kpe2[:, :HEAD_PAD]
    kp2 = jnp.sum(kpe_raw * kpe_raw, axis=-1, keepdims=True)
    kpe_rot = kpe_raw * (gkr_ref[...] * cos) + kpe2[:, HEAD_PAD:] * (gkrs_ref[...] * sin)
    kpe_ref[...] = kpe_rot[:, :QK_ROPE].reshape(bb, tt, QK_ROPE)
    knf = _dot(cb, wuk_ref[...])
    if emit_kv:
        v_ref[...] = (_dot(cb, wuv_ref[...]) + vone_ref[...]).reshape(bb, tt, QKV_WIDTH).astype(BF16)

    q_cos = gq_ref[...] * cos
    q_sin = gqs_ref[...] * sin
    for h in range(MLA_HEADS):
        sl = slice(h * HEAD_PAD, (h + 1) * HEAD_PAD)
        qh = qf[:, sl]
        ms = jnp.sum(qh * qh, axis=-1, keepdims=True) * (1.0 / QK_DIM)
        qh = lax.rsqrt(ms + RMS_EPS) * (qh * q_cos + qs[:, sl] * q_sin)
        q_ref[:, :, sl] = qh.reshape(bb, tt, HEAD_PAD).astype(q_ref.dtype)
        gates_due(h)

    gkn = gkn_ref[...]
    head_lane = lax.broadcasted_iota(jnp.int32, (tm, MLA_HEADS), 1)
    kinv_all = jnp.zeros((tm, MLA_HEADS), F32)
    for h in range(MLA_HEADS):
        sl = slice(h * HEAD_PAD, (h + 1) * HEAD_PAD)
        kn = knf[:, sl]
        ms = (jnp.sum(kn * kn, axis=-1, keepdims=True) + kp2) * (1.0 / QK_DIM)
        kinv = lax.rsqrt(ms + RMS_EPS)
        kinv_all = jnp.where(head_lane == h, kinv, kinv_all)
        if emit_kv:
            k_ref[:, :, sl] = ((kn * gkn + kpe_rot) * kinv).reshape(bb, tt, HEAD_PAD).astype(BF16)
        gates_due(MLA_HEADS + h)
    kinv_ref[...] = kinv_all.reshape(bb, tt, MLA_HEADS)


def _front_mla(x, mod, tabs, w, q_gain, *, bb, tt, q_dtype, emit_kv):
    nb, t_all, _ = x.shape
    grid = (nb // bb, t_all // tt)
    tok = lambda width: pl.BlockSpec((bb, tt, width), lambda i, j: (i, j, 0))
    tab = pl.BlockSpec((tt, HEAD_PAD), lambda i, j: (j, 0))
    weights = [w["w_qa"], w["w_kva"], w["w_kpe"], w["w_za"], w["w_ga"], w["w_gb"], w["g_qa"], w["w_uq"],
               w["w_uq_swap"], w["g_q"] * q_gain, w["g_q_swap"] * q_gain, w["g_kva"], w["w_uk"], w["g_k_rope"],
               w["g_k_rope_swap"], w["g_k_nope"], w["w_uv"], w["v_one"]]
    in_specs = ([tok(D_MODEL), pl.BlockSpec((bb, 1, 3 * D_MODEL), lambda i, j: (i, 0, 0)),
                 _const_spec((1, 1, D_MODEL)), tab, tab] + [_const_spec(a.shape) for a in weights])
    sds = lambda width, dt=F32: jax.ShapeDtypeStruct((nb, t_all, width), dt)
    out_shape = [sds(QKV_WIDTH, q_dtype), sds(KV_LORA), sds(QK_ROPE), sds(MLA_HEADS), sds(MLA_WIDTH), sds(D_MODEL),
                 sds(D_MODEL)]
    out_specs = [tok(QKV_WIDTH), tok(KV_LORA), tok(QK_ROPE), tok(MLA_HEADS), tok(MLA_WIDTH), tok(D_MODEL),
                 tok(D_MODEL)]
    if emit_kv:
        out_shape += [sds(QKV_WIDTH, BF16), sds(QKV_WIDTH, BF16)]
        out_specs += [tok(QKV_WIDTH), tok(QKV_WIDTH)]
    return pl.pallas_call(
        functools.partial(_front_mla_kernel, bb=bb, tt=tt, emit_kv=emit_kv),
        grid=grid, in_specs=in_specs, out_specs=out_specs, out_shape=out_shape,
        name="front_mla", compiler_params=_params("parallel", "arbitrary"),
    )(x, mod, w["g_norm"], *tabs, *weights)


def _front_rwkv_kernel(x_ref, mod_ref, gn_ref, s0r_ref, s0w_ref, s0k_ref, s0v_ref, s0a_ref,
                       mur_ref, muw_ref, muk_ref, muv_ref, mua_ref,
                       wr_ref, ww_ref, wk_ref, wv_ref, wa_ref, wzb_ref,
                       wd2_ref, w0_ref, wa2_ref, a0_ref,
                       r_ref, k_ref, v_ref, ld_ref, a_ref, zb_ref,
                       lr_ref, lw_ref, lk_ref, lv_ref, la_ref,
                       cr_ref, cw_ref, ck_ref, cv_ref, ca_ref, *, bb, tt):
    j = pl.program_id(1)
    hb = _modulated(x_ref, mod_ref, gn_ref, bb, tt)

    def shifted(w_ref, s0_ref, mu_ref, last_ref, carry_ref):
        width = w_ref.shape[1]
        p = _dot(hb, w_ref[...]).reshape(bb, tt, width)
        first = jnp.where(j == 0, s0_ref[...], carry_ref[...])
        t_idx = lax.broadcasted_iota(jnp.int32, (bb, tt, width), 1)
        prev = jnp.where(t_idx == 0, first, pltpu.roll(p, 1, 1))
        last = p[:, tt - 1:tt, :]
        last_ref[...] = last
        carry_ref[...] = last
        return (p + (prev - p) * mu_ref[...]).reshape(bb * tt, width)

    r_ref[...] = shifted(wr_ref, s0r_ref, mur_ref, lr_ref, cr_ref).reshape(bb, tt, RW_WIDTH)
    k_ref[...] = shifted(wk_ref, s0k_ref, muk_ref, lk_ref, ck_ref).reshape(bb, tt, RW_WIDTH)
    v_ref[...] = shifted(wv_ref, s0v_ref, muv_ref, lv_ref, cv_ref).reshape(bb, tt, RW_WIDTH)
    wl = shifted(ww_ref, s0w_ref, muw_ref, lw_ref, cw_ref)
    al = shifted(wa_ref, s0a_ref, mua_ref, la_ref, ca_ref)
    y = w0_ref[...] + _dot(jnp.tanh(wl), wd2_ref[...])
    ld_ref[...] = (-DECAY_SCALE * _sigmoid(y)).reshape(bb, tt, RW_WIDTH)
    a_ref[...] = _sigmoid(a0_ref[...] + _dot(al, wa2_ref[...])).reshape(bb, tt, RW_WIDTH)
    zb_ref[...] = _dot(hb, wzb_ref[...]).reshape(bb, tt, RW_WIDTH)


def _front_rwkv(x, mod, s0, w, *, bb, tt):
    nb, t_all, _ = x.shape
    grid = (nb // bb, t_all // tt)
    tok = lambda width: pl.BlockSpec((bb, tt, width), lambda i, j: (i, j, 0))
    row = lambda width: pl.BlockSpec((bb, 1, width), lambda i, j: (i, 0, 0))
    widths = (RW_WIDTH, LANES, RW_WIDTH, RW_WIDTH, LANES)
    mus = [w["mu_r"], w["mu_w"], w["mu_k"], w["mu_v"], w["mu_a"]]
    weights = [w["w_r"], w["w_wl"], w["w_k"], w["w_v"], w["w_al"], w["w_zb"], w["w_decay2"], w["w0"], w["w_a2"],
               w["a0"]]
    in_specs = ([tok(D_MODEL), row(3 * D_MODEL), _const_spec((1, 1, D_MODEL))] + [row(n) for n in widths]
                + [_const_spec(a.shape) for a in mus] + [_const_spec(a.shape) for a in weights])
    out_shape = ([jax.ShapeDtypeStruct((nb, t_all, RW_WIDTH), F32)] * 6
                 + [jax.ShapeDtypeStruct((nb, 1, n), F32) for n in widths])
    out_specs = [tok(RW_WIDTH)] * 6 + [row(n) for n in widths]
    scratch = [pltpu.VMEM((bb, 1, n), F32) for n in widths]
    return pl.pallas_call(
        functools.partial(_front_rwkv_kernel, bb=bb, tt=tt),
        grid=grid, in_specs=in_specs, out_specs=out_specs, out_shape=out_shape, scratch_shapes=scratch,
        name="front_rwkv", compiler_params=_params("parallel", "arbitrary"),
    )(x, mod, w["g_norm"], *s0, *mus, *weights)


def _attn_prompt_kernel(q_ref, k_ref, v_ref, za_ref, o_ref, m_ref, acc_ref, *, tq):
    qi = pl.program_id(1)
    row = lax.broadcasted_iota(jnp.int32, (tq, tq), 0)
    col = lax.broadcasted_iota(jnp.int32, (tq, tq), 1)
    causal = col <= row
    m_ref[...] = jnp.full(m_ref.shape, NEG_BIG, F32)
    acc_ref[...] = jnp.zeros(acc_ref.shape, F32)

    def step(j, masked):
        rows = pl.ds(pl.multiple_of(j * tq, tq), tq)
        for h0 in range(0, MLA_HEADS, HEADS_PER_STAGE):
            heads = list(range(h0, h0 + HEADS_PER_STAGE))
            sls = [slice(h * HEAD_PAD, (h + 1) * HEAD_PAD) for h in heads]
            s = [_dot_nt(q_ref[0, :, sl], k_ref[0, rows, sl]) for sl in sls]
            if masked:
                s = [jnp.where(causal, x, NEG_BIG) for x in s]
            m_old = [m_ref[h] for h in heads]
            m_new = [jnp.maximum(mo, jnp.max(x, axis=-1, keepdims=True)) for mo, x in zip(m_old, s)]
            p = [jnp.exp2(jnp.concatenate([x[:, i:i + LANES] - mn for i in range(0, tq, LANES)], axis=1))
                 for x, mn in zip(s, m_new)]
            pv = [_dot(x, v_ref[0, rows, sl]) for x, sl in zip(p, sls)]
            for h, mo, mn, x in zip(heads, m_old, m_new, pv):
                acc_ref[h] = jnp.exp2(mo - mn) * acc_ref[h] + x
                m_ref[h] = mn

    def full_block(j, carry):
        step(j, False)
        return carry

    lax.fori_loop(0, qi, full_block, 0)
    step(qi, True)

    low_half = lax.broadcasted_iota(jnp.int32, (tq, LANES), 1) < V_DIM
    for pair in range(MLA_HEADS // 2):
        even, odd = acc_ref[2 * pair], acc_ref[2 * pair + 1]
        o = jnp.where(low_half, even / even[:, V_DIM:V_DIM + 1], odd / odd[:, 0:1])
        sl = slice(pair * LANES, (pair + 1) * LANES)
        o_ref[0, :, sl] = (o * _silu(za_ref[0, :, sl])).astype(BF16)


def _attn_prompt(q, k, v, za, *, tq):
    nb, s_len, _ = q.shape
    return pl.pallas_call(
        functools.partial(_attn_prompt_kernel, tq=tq),
        grid=(nb, s_len // tq),
        in_specs=[pl.BlockSpec((1, tq, QKV_WIDTH), lambda b, i: (b, i, 0)),
                  pl.BlockSpec((1, s_len, QKV_WIDTH), lambda b, i: (b, 0, 0)),
                  pl.BlockSpec((1, s_len, QKV_WIDTH), lambda b, i: (b, 0, 0)),
                  pl.BlockSpec((1, tq, MLA_WIDTH), lambda b, i: (b, i, 0))],
        out_specs=pl.BlockSpec((1, tq, MLA_WIDTH), lambda b, i: (b, i, 0)),
        out_shape=jax.ShapeDtypeStruct((nb, s_len, MLA_WIDTH), BF16),
        scratch_shapes=[pltpu.VMEM((MLA_HEADS, tq, LANES), F32), pltpu.VMEM((MLA_HEADS, tq, HEAD_PAD), F32)],
        name="attn_prompt", compiler_params=_params("parallel", "arbitrary"),
    )(q, k, v, za)


def _attn_sample_kernel(pt_ref, q_ref, kvn_ref, kpn_ref, kin_ref, za_ref, wukt_ref, gkn_ref, wuv_ref, *rest,
                        pp, tt):
    pages = rest[:3 * pp]
    o_ref = rest[3 * pp]
    qlat_ref, qpe_ref, m_ref, l_ref, acc_ref = rest[3 * pp + 1:]
    j = pl.program_id(1)
    rows = MLA_HEADS * tt

    @pl.when(j == 0)
    def _():
        gkn = gkn_ref[...]
        for h in range(MLA_HEADS):
            qh = q_ref[0, :, h * HEAD_PAD:(h + 1) * HEAD_PAD]
            qlat_ref[h * tt:(h + 1) * tt, :] = _dot(qh * gkn, wukt_ref[h])
            qpe_ref[h * tt:(h + 1) * tt, :] = qh[:, :QK_ROPE]
        m_ref[...] = jnp.full((rows, 1), NEG_BIG, F32)
        l_ref[...] = jnp.zeros((rows, 1), F32)
        acc_ref[...] = jnp.zeros((rows, KV_LORA), F32)

    qlat = qlat_ref[...].astype(BF16)
    qpe = qpe_ref[...].astype(BF16)

    def update(scores, values):
        m_old = m_ref[...]
        top = scores[0]
        for s in scores[1:]:
            top = jnp.maximum(top, s)
        m_new = jnp.maximum(m_old, jnp.max(top, axis=-1, keepdims=True))
        alpha = jnp.exp(m_old - m_new)
        ps = [jnp.exp(s - m_new) for s in scores]
        tot, pv = ps[0], _dot(ps[0], values[0])
        for p, c in zip(ps[1:], values[1:]):
            tot = tot + p
            pv = pv + _dot(p, c)
        l_ref[...] = alpha * l_ref[...] + jnp.sum(tot, axis=-1, keepdims=True)
        acc_ref[...] = alpha * acc_ref[...] + pv
        m_ref[...] = m_new

    values = [pages[3 * i][0].astype(BF16) for i in range(pp)]
    lat = [_dot_nt(qlat, cb) for cb in values]
    pe = [_dot(qpe, pages[3 * i + 1][0]) for i in range(pp)]
    scores = [((x + y).reshape(MLA_HEADS, tt, PAGE_SIZE) * pages[3 * i + 2][0][:, None, :]).reshape(rows, PAGE_SIZE)
              for i, (x, y) in enumerate(zip(lat, pe))]
    per = max(1, pp // SOFTMAX_STEPS)
    for i in range(0, pp, per):
        update(scores[i:i + per], values[i:i + per])

    @pl.when(j == pl.num_programs(1) - 1)
    def _():
        pad = lambda x: jnp.concatenate([x, jnp.zeros((PAGE_SIZE - tt, x.shape[1]), F32)], axis=0)
        key = lax.broadcasted_iota(jnp.int32, (rows, PAGE_SIZE), 1)
        tok = lax.broadcasted_iota(jnp.int32, (rows, PAGE_SIZE), 0) % tt
        expand = (lax.broadcasted_iota(jnp.int32, (rows, MLA_HEADS), 0) // tt
                  == lax.broadcasted_iota(jnp.int32, (rows, MLA_HEADS), 1)).astype(BF16)
        cn = pad(kvn_ref[0]).astype(BF16)
        k_hi, k_mid, k_lo = _split3(pad(kin_ref[0]))
        s = _dot_nt(qlat, cn) + _dot_nt(qpe, pad(kpn_ref[0]))
        s = s * (_dot_nt(expand, k_hi) + _dot_nt(expand, k_mid) + _dot_nt(expand, k_lo))
        update([jnp.where(key <= tok, s, NEG_BIG)], [cn])
        o_lat = (acc_ref[...] / l_ref[...]).astype(BF16)
        for pair in range(MLA_HEADS // 2):
            o = jnp.zeros((tt, LANES), F32)
            for h in (2 * pair, 2 * pair + 1):
                o = o + _dot(o_lat[h * tt:(h + 1) * tt, :], wuv_ref[:, h * HEAD_PAD:(h + 1) * HEAD_PAD])
            sl = slice(pair * LANES, (pair + 1) * LANES)
            o_ref[0, :, sl] = (o * _silu(za_ref[0, :, sl])).astype(BF16)


def _attn_sample(page_table, q, kv_new, kpe_new, kinv_new, za, cache_kv, cache_kpe, cache_kinv, w, *, pp):
    nb, tt, _ = q.shape
    n_pages = page_table.shape[1]
    rows = MLA_HEADS * tt
    per_b = lambda width: pl.BlockSpec((1, tt, width), lambda b, j, pt: (b, 0, 0))
    const = lambda shape: pl.BlockSpec(shape, lambda b, j, pt: (0,) * len(shape))

    def page(shape, i):
        return pl.BlockSpec((1,) + shape, lambda b, j, pt: (pt[b, j * pp + i], 0, 0))

    kpe_t = jnp.swapaxes(cache_kpe, 1, 2)
    kinv_t = jnp.swapaxes(cache_kinv, 1, 2)
    page_specs, page_args = [], []
    for i in range(pp):
        page_specs += [page((PAGE_SIZE, KV_LORA), i), page((QK_ROPE, PAGE_SIZE), i), page((MLA_HEADS, PAGE_SIZE), i)]
        page_args += [cache_kv, kpe_t, kinv_t]
    grid_spec = pltpu.PrefetchScalarGridSpec(
        num_scalar_prefetch=1, grid=(nb, n_pages // pp),
        in_specs=[per_b(QKV_WIDTH), per_b(KV_LORA), per_b(QK_ROPE), per_b(MLA_HEADS), per_b(MLA_WIDTH),
                  const(w["w_ukt"].shape), const(w["g_k_nope"].shape), const(w["w_uv"].shape)] + page_specs,
        out_specs=per_b(MLA_WIDTH),
        scratch_shapes=[pltpu.VMEM((rows, KV_LORA), F32), pltpu.VMEM((rows, QK_ROPE), F32),
                        pltpu.VMEM((rows, 1), F32), pltpu.VMEM((rows, 1), F32), pltpu.VMEM((rows, KV_LORA), F32)])
    return pl.pallas_call(
        functools.partial(_attn_sample_kernel, pp=pp, tt=tt),
        grid_spec=grid_spec,
        out_shape=jax.ShapeDtypeStruct((nb, tt, MLA_WIDTH), BF16),
        name="attn_sample", compiler_params=_params("parallel", "arbitrary"),
    )(page_table, q, kv_new, kpe_new, kinv_new, za, w["w_ukt"], w["g_k_nope"], w["w_uv"], *page_args)


def _wkv_kernel(*refs, tk, n_chunks, has_state):
    if has_state:
        (r_ref, k_ref, v_ref, ld_ref, a_ref, zb_ref, kk_ref, ka_ref, rk_ref, gg_ref, gb_ref, s0_ref,
         o_ref, so_ref, s_ref) = refs
    else:
        (r_ref, k_ref, v_ref, ld_ref, a_ref, zb_ref, kk_ref, ka_ref, rk_ref, gg_ref, gb_ref,
         o_ref, so_ref, s_ref) = refs
    C, L, G = WKV_CHUNK, GROUP_LANES, WKV_GROUP
    GC = G * C
    n_groups = RW_WIDTH // L
    ti = pl.program_id(1)
    ri = lax.broadcasted_iota(jnp.int32, (GC, L), 0)
    ci = lax.broadcasted_iota(jnp.int32, (GC, L), 1)
    blk = lambda n: (ri // n) == (ci // n)
    same_head = blk(RW_HEAD).astype(F32)
    same_head_bf = same_head.astype(BF16)
    t2 = lax.broadcasted_iota(jnp.int32, (C, GC), 0)
    s2 = lax.broadcasted_iota(jnp.int32, (C, GC), 1) % C
    strict2 = (s2 < t2).astype(F32)
    incl2 = (s2 <= t2).astype(F32)
    tri2 = jnp.concatenate([strict2, incl2], axis=0)
    blk2 = lambda n: (t2 // n) == (s2 // n)
    neg_m16 = -blk2(16).astype(F32)
    m32_off = (blk2(32) & ~blk2(16)).astype(F32)
    m64_off = (~blk2(32)).astype(F32)
    eye2 = (t2 == s2).astype(F32)
    tri = (lax.broadcasted_iota(jnp.int32, (C, C), 1) <= lax.broadcasted_iota(jnp.int32, (C, C), 0)).astype(BF16)
    ones_bd = same_head.astype(BF16)
    replicate = (lax.broadcasted_iota(jnp.int32, (RW_HEAD, L), 1) % RW_HEAD
                 == lax.broadcasted_iota(jnp.int32, (RW_HEAD, L), 0)).astype(BF16)

    def seg_sum(xs, split):
        n = xs[0].shape[0]
        if not split:
            y = _dot(jnp.concatenate(xs, axis=0), ones_bd)
            return [y[i * n:(i + 1) * n] for i in range(len(xs))]
        parts = [piece for x in xs for piece in _split2(x)]
        y = jnp.dot(jnp.concatenate(parts, axis=0), ones_bd, preferred_element_type=F32)
        return [y[2 * i * n:(2 * i + 1) * n] + y[(2 * i + 1) * n:(2 * i + 2) * n] for i in range(len(xs))]

    def expand(x):
        return jnp.concatenate([x.astype(BF16)] * G, axis=0) * same_head_bf

    stack = expand

    @pl.when(ti == 0)
    def _():
        if has_state:
            for g in range(n_groups):
                hi, mid, lo = _split3(s0_ref[0, g * L:(g + 1) * L, :])
                s_ref[g] = (_dot(hi, replicate) + _dot(mid, replicate) + _dot(lo, replicate)) * same_head
        else:
            s_ref[...] = jnp.zeros(s_ref.shape, F32)

    def each(f, *cols):
        return [f(*args) for args in zip(*cols)]

    groups = list(range(n_groups))
    lanes = [slice(g * L, (g + 1) * L) for g in groups]
    cpi = 4 if n_chunks % 4 == 0 else 1

    def step(i, carry):
        if tk == C:
            rows = [pl.ds(pl.multiple_of((i * cpi + j) * C, C), C) for j in range(cpi)]
            load = lambda ref: [ref[0, rw, ln] for rw in rows for ln in lanes]
        else:
            rows = [slice(0, tk)]
            load = lambda ref: [jnp.concatenate([ref[0, :, ln], jnp.zeros((C - tk, L), F32)], axis=0) for ln in lanes]
        r, k, v, ld, a, zb = (load(ref) for ref in (r_ref, k_ref, v_ref, ld_ref, a_ref, zb_ref))
        k_k, k_a, r_k, gn_g, gn_b = ([ref[:, ln] for _ in rows for ln in lanes]
                                     for ref in (kk_ref, ka_ref, rk_ref, gg_ref, gb_ref))
        ld_split = each(_split2, ld)
        cum = each(lambda hl: _dot(tri, hl[0]) + _dot(tri, hl[1]), ld_split)
        cum_end = each(lambda x: x[C - 1:C, :], cum)
        kk = each(lambda x, y: x * y, k, k_k)
        norm = seg_sum(each(lambda x: x * x, kk), split=True)
        kk = each(lambda x, n: x * lax.rsqrt(jnp.maximum(n, 1e-24)), kk, norm)
        k = each(lambda x, a_, ka: x * (1.0 + (a_ - 1.0) * ka), k, a, k_a)
        b = each(lambda x, y: x * y, kk, a)
        grow = each(lambda x: jnp.exp(-x), cum)
        tail = each(lambda e, x: jnp.exp(e - x), cum_end, cum)
        ar = each(lambda kk_, x, l_, r_: jnp.concatenate([kk_ * jnp.exp(x - l_), r_ * jnp.exp(x)], axis=0).astype(BF16),
                  kk, cum, ld, r)
        bk = each(lambda b_, k_, g_: jnp.concatenate([stack(b_ * g_), stack(k_ * g_)], axis=0),
                  b, k, grow)
        gm = each(_dot_nt, ar, bk)
        vs = each(stack, v)
        l2b = each(lambda m: m[:C, :GC] * strict2, gm)
        kv_part = each(lambda m, vs_: _dot(m[:, GC:] * tri2, vs_), gm, vs)
        m2b = each(lambda m: (m[C:, :GC] * incl2).astype(BF16), gm)
        kb = each(lambda k_, b_, t_: jnp.concatenate([k_ * t_, b_ * t_], axis=0).astype(BF16), k, b, tail)
        decay_end = each(jnp.exp, cum_end)
        rkr = each(lambda r_, k_, rk: r_ * k_ * rk, r, k, r_k)
        p = each(lambda l_: l_ * neg_m16, l2b)
        x = each(lambda p_: eye2 + p_, p)
        p = each(lambda p_: _dot(p_, expand(p_)), p)
        for _ in range(2):
            t = each(lambda x_, p_: _dot(jnp.concatenate([x_, p_], axis=0), expand(p_)), x, p)
            x = each(lambda x_, t_: x_ + t_[:C], x, t)
            p = each(lambda t_: t_[C:], t)
        x = each(lambda x_, p_: x_ + _dot(x_, expand(p_)), x, p)
        for off in (m32_off, m64_off):
            y = each(lambda l_, x_: _dot(l_ * off, expand(x_)), l2b, x)
            x = each(lambda x_, y_: x_ - _dot(x_, expand(y_)), x, y)
        x2 = each(lambda x_: x_.astype(BF16), x)

        o = []
        for j in range(len(rows)):
            part = slice(j * n_groups, (j + 1) * n_groups)
            s_prev = [s_ref[g] for g in groups]
            ars = each(_dot_nt, ar[part], s_prev)
            rhs = each(lambda s_, kv_: s_[:C] + kv_[:C], ars, kv_part[part])
            u = each(lambda x_, rhs_: _dot(x_, stack(rhs_)), x2[part], rhs)
            o += each(lambda s_, kv_, m_, u_: s_[C:] + kv_[C:] - _dot(m_, stack(u_)), ars, kv_part[part], m2b[part], u)
            ds = each(lambda v_, u_, kb_: _dot(jnp.concatenate([v_, -u_], axis=0).T, kb_), v[part], u, kb[part])
            for g, s_, d_, ds_ in zip(groups, s_prev, decay_end[part], ds):
                s_ref[g] = s_ * d_ + ds_ * same_head

        sums = seg_sum(o + rkr, split=False)
        mean, bonus = sums[:len(o)], sums[len(o):]
        dev = each(lambda x_, m: x_ - m * (1.0 / RW_HEAD), o, mean)
        var = seg_sum(each(lambda x_: x_ * x_, dev), split=False)
        for n, (rw, ln) in enumerate((rw, ln) for rw in rows for ln in lanes):
            out = dev[n] * lax.rsqrt(var[n] * (1.0 / RW_HEAD) + GN_EPS) * gn_g[n] + gn_b[n] + bonus[n] * v[n]
            out = (out * _silu(zb[n])).astype(BF16)
            o_ref[0, rw, ln] = out[:tk] if tk != C else out
        return carry

    if tk != C:
        step(0, 0)
    else:
        lax.fori_loop(0, n_chunks // cpi, step, 0)

    @pl.when(ti == pl.num_programs(1) - 1)
    def _():
        rep_t = (lax.broadcasted_iota(jnp.int32, (L, RW_HEAD), 0) % RW_HEAD
                 == lax.broadcasted_iota(jnp.int32, (L, RW_HEAD), 1)).astype(BF16)
        for g in range(n_groups):
            hi, mid, lo = _split3(s_ref[g])
            so_ref[0, g * L:(g + 1) * L, :] = _dot(hi, rep_t) + _dot(mid, rep_t) + _dot(lo, rep_t)


def _wkv(r, k, v, ld, a, zb, w, state):
    nb, t_all, _ = r.shape
    has_state = state is not None
    if has_state:
        tk, tb = t_all, t_all
    else:
        tk, tb = WKV_CHUNK, _tile(t_all, 512)
    n_groups = RW_WIDTH // GROUP_LANES
    seq = pl.BlockSpec((1, tb, RW_WIDTH), lambda b, t: (b, t, 0))
    vec = pl.BlockSpec((1, RW_WIDTH), lambda b, t: (0, 0))
    st = pl.BlockSpec((1, RW_WIDTH, RW_HEAD), lambda b, t: (b, 0, 0))
    in_specs = [seq] * 6 + [vec] * 5 + ([st] if has_state else [])
    args = [r, k, v, ld, a, zb, w["k_k"], w["k_a"], w["r_k"], w["gn_g"], w["gn_b"]] + ([state] if has_state else [])
    return pl.pallas_call(
        functools.partial(_wkv_kernel, tk=tk, n_chunks=tb // tk, has_state=has_state),
        grid=(nb, t_all // tb), in_specs=in_specs, out_specs=[seq, st],
        out_shape=[jax.ShapeDtypeStruct((nb, t_all, RW_WIDTH), BF16),
                   jax.ShapeDtypeStruct((nb, RW_WIDTH, RW_HEAD), F32)],
        scratch_shapes=[pltpu.VMEM((n_groups, GROUP_LANES, GROUP_LANES), F32)],
        name="wkv", compiler_params=_params("parallel", "arbitrary"),
    )(*args)


def _merge_kernel(x_ref, mod_ref, oa_ref, ob_ref, ga_ref, gb_ref, wpa_ref, wpb_ref, wout_ref, y_ref, *, bb, tt):
    tm = bb * tt
    flat = lambda ref: ref[...].reshape(tm, ref.shape[2])
    m = (_sigmoid(flat(ga_ref)) * _dot(flat(oa_ref), wpa_ref[...])
         + _sigmoid(flat(gb_ref)) * _dot(flat(ob_ref), wpb_ref[...]))
    gate = mod_ref[...][:, :, 2 * D_MODEL:]
    y_ref[...] = x_ref[...] + gate * _dot(m, wout_ref[...]).reshape(bb, tt, D_MODEL)


def _merge(x, mod, oa, ob, ga, gb, w, *, bb, tt):
    nb, t_all, _ = x.shape
    tok = pl.BlockSpec((bb, tt, D_MODEL), lambda i, j: (i, j, 0))
    return pl.pallas_call(
        functools.partial(_merge_kernel, bb=bb, tt=tt),
        grid=(nb // bb, t_all // tt),
        in_specs=[tok, pl.BlockSpec((bb, 1, 3 * D_MODEL), lambda i, j: (i, 0, 0)), tok, tok, tok, tok,
                  _const_spec(w["w_pa"].shape), _const_spec(w["w_pb"].shape), _const_spec(w["w_out"].shape)],
        out_specs=tok, out_shape=jax.ShapeDtypeStruct(x.shape, F32),
        name="merge", compiler_params=_params("parallel", "parallel"),
    )(x, mod, oa, ob, ga, gb, w["w_pa"], w["w_pb"], w["w_out"])


def _head_slabs(rope_part, nope_part):
    lead = nope_part.shape[:-1]
    parts = []
    if rope_part is None:
        parts.append(jnp.zeros(lead + (NOPE_OFF,), nope_part.dtype))
    else:
        parts += [rope_part, jnp.zeros(lead + (NOPE_OFF - QK_ROPE,), nope_part.dtype)]
    parts.append(nope_part)
    slab = jnp.concatenate(parts, axis=-1)
    return slab.reshape(lead[:-1] + (MLA_HEADS * HEAD_PAD,))


def _pad_cols(x, width):
    return jnp.pad(x, ((0, 0), (0, width - x.shape[1])))


def _prepare_weights(w_ada, b_ada, g_norm, w_in, g_qa, w_uq, g_kva, w_uk, w_uv, g_q, g_k, w_pa, mu_shift, w0,
                     w_decay2, a0, w_a2, k_k, k_a, r_k, gn_g, gn_b, w_pb, w_out):
    bf = lambda x: x.astype(BF16)
    row = lambda x: x.reshape(1, -1)
    splits = (Q_LORA, KV_LORA, QK_ROPE, MLA_WIDTH, RW_WIDTH, DECAY_LORA, RW_WIDTH, RW_WIDTH, A_LORA, RW_WIDTH,
              D_MODEL, D_MODEL)
    offs = [0]
    for s in splits:
        offs.append(offs[-1] + s)
    cols = [w_in[:, offs[i]:offs[i + 1]] for i in range(len(splits))]
    w_qa, w_kva, w_kpe, w_za, w_r, w_wl, w_k, w_v, w_al, w_zb, w_ga, w_gb = cols
    mu = [mu_shift[offs[i] - offs[4]:offs[i + 1] - offs[4]] for i in range(4, 9)]
    pad_row = lambda x: _pad_cols(row(x), LANES).reshape(1, 1, LANES)
    row3 = lambda x: x.reshape(1, 1, -1)
    zeros_h = jnp.zeros((KV_LORA, MLA_HEADS, V_DIM), F32)
    parity = (jnp.arange(MLA_HEADS) % 2 == 0)[None, :, None]
    w_uv_slab = jnp.concatenate([jnp.where(parity, w_uv, zeros_h), jnp.where(parity, zeros_h, w_uv)], axis=-1)
    w_ukt = jnp.pad(jnp.transpose(w_uk, (1, 2, 0)), ((0, 0), (NOPE_OFF, 0), (0, 0)))
    swap_cols = lambda x: jnp.concatenate([-x[..., HALF_ROPE:], x[..., :HALF_ROPE]], axis=-1)
    swap_gain = lambda g: jnp.pad(jnp.concatenate([g[HALF_ROPE:], g[:HALF_ROPE]]), (0, HEAD_PAD - QK_ROPE))
    w_uq_swap = _head_slabs(swap_cols(w_uq[..., QK_NOPE:]), jnp.zeros_like(w_uq[..., :QK_NOPE]))
    w_kpe_both = jnp.concatenate([_pad_cols(w_kpe, HEAD_PAD), _pad_cols(swap_cols(w_kpe), HEAD_PAD)], axis=1)
    g_k_rope = jnp.pad(g_k[QK_NOPE:], (0, HEAD_PAD - QK_ROPE))
    g_k_nope = jnp.pad(g_k[:QK_NOPE], (NOPE_OFF, 0))
    g_q_slab = jnp.concatenate([g_q[QK_NOPE:], jnp.zeros((NOPE_OFF - QK_ROPE,), F32), g_q[:QK_NOPE]])
    lane = jnp.arange(QKV_WIDTH) % HEAD_PAD
    even = (jnp.arange(QKV_WIDTH) // HEAD_PAD) % 2 == 0
    v_one = jnp.where(even, lane == V_DIM, lane == 0).astype(F32)
    return {
        "w_ada": bf(w_ada), "b_ada": row(b_ada), "g_norm": row3(g_norm),
        "w_qa": bf(w_qa), "w_kva": bf(w_kva), "w_kpe": bf(w_kpe_both), "w_za": bf(w_za),
        "w_ga": bf(w_ga), "w_gb": bf(w_gb), "g_qa": row(g_qa), "g_kva": row(g_kva),
        "w_uq": bf(_head_slabs(w_uq[..., QK_NOPE:], w_uq[..., :QK_NOPE])),
        "w_uk": bf(_head_slabs(None, w_uk)), "w_uv": bf(w_uv_slab.reshape(KV_LORA, QKV_WIDTH)),
        "w_uq_swap": bf(w_uq_swap), "w_ukt": bf(w_ukt), "g_q": row(g_q_slab),
        "g_q_swap": row(swap_gain(g_q[QK_NOPE:])), "v_one": row(v_one),
        "g_k_rope": row(g_k_rope), "g_k_rope_swap": row(swap_gain(g_k[QK_NOPE:])), "g_k_nope": row(g_k_nope),
        "w_r": bf(w_r), "w_wl": bf(_pad_cols(w_wl, LANES)), "w_k": bf(w_k), "w_v": bf(w_v),
        "w_al": bf(_pad_cols(w_al, LANES)), "w_zb": bf(w_zb),
        "mu_r": row3(mu[0]), "mu_w": pad_row(mu[1]), "mu_k": row3(mu[2]), "mu_v": row3(mu[3]), "mu_a": pad_row(mu[4]),
        "w_decay2": bf(jnp.pad(w_decay2, ((0, LANES - DECAY_LORA), (0, 0)))), "w0": row(w0),
        "w_a2": bf(jnp.pad(w_a2, ((0, LANES - A_LORA), (0, 0)))), "a0": row(a0),
        "k_k": row(k_k), "k_a": row(k_a), "r_k": row(r_k), "gn_g": row(gn_g), "gn_b": row(gn_b),
        "w_pa": bf(w_pa), "w_pb": bf(w_pb), "w_out": bf(w_out),
    }


def _rope_tables(pos):
    inv = ROPE_THETA ** (-jnp.arange(HALF_ROPE, dtype=F32) / HALF_ROPE)
    ang = pos.astype(F32)[:, None] * inv[None, :]
    cos, sin = jnp.cos(ang), jnp.sin(ang)
    n = pos.shape[0]
    rest = jnp.zeros((n, HEAD_PAD - QK_ROPE), F32)
    return jnp.concatenate([cos, cos, rest + 1.0], axis=1), jnp.concatenate([sin, sin, rest], axis=1)


def _split_shift(state_shift):
    o1 = RW_WIDTH
    o2 = o1 + DECAY_LORA
    o3 = o2 + RW_WIDTH
    o4 = o3 + RW_WIDTH
    n = state_shift.shape[0]
    pad = lambda x: _pad_cols(x, LANES)
    parts = (state_shift[:, :o1], pad(state_shift[:, o1:o2]), state_shift[:, o2:o3], state_shift[:, o3:o4],
             pad(state_shift[:, o4:]))
    return [p.reshape(n, 1, -1) for p in parts]


def _join_shift(last):
    lr, lw, lk, lv, la = (x[:, 0, :] for x in last)
    return jnp.concatenate([lr, lw[:, :DECAY_LORA], lk, lv, la[:, :A_LORA]], axis=1)


def _tile(n, target):
    t = min(n, target)
    while n % t:
        t //= 2
    return t


def kernel(x_prompt, x_sample, c_prompt, c_sample, cache_kv, cache_kpe, cache_kinv, state_wkv, state_shift, page_table, w_ada, b_ada, g_norm, w_in, g_qa, w_uq, g_kva, w_uk, w_uv, g_q, g_k, w_pa, mu_shift, w0, w_decay2, a0, w_a2, k_k, k_a, r_k, gn_g, gn_b, w_pb, w_out):
    B, S, _ = x_prompt.shape
    Bd, T, _ = x_sample.shape
    n_pages = page_table.shape[1]
    past = n_pages * PAGE_SIZE
    w = _prepare_weights(w_ada, b_ada, g_norm, w_in, g_qa, w_uq, g_kva, w_uk, w_uv, g_q, g_k, w_pa, mu_shift, w0,
                         w_decay2, a0, w_a2, k_k, k_a, r_k, gn_g, gn_b, w_pb, w_out)

    mod = _mod(jnp.concatenate([c_prompt, c_sample], axis=0), w["w_ada"], w["b_ada"])
    mod_p = mod[:B].reshape(B, 1, 3 * D_MODEL)
    mod_s = mod[B:].reshape(Bd, 1, 3 * D_MODEL)

    tt = _tile(S, 256)
    q, kv_p, kpe_p, kinv_p, za, ga, gb, keys, vals = _front_mla(
        x_prompt, mod_p, _rope_tables(jnp.arange(S)), w, SCALE * LOG2_E, bb=1, tt=tt, q_dtype=BF16, emit_kv=True)
    zero_shift = _split_shift(jnp.zeros((B, 3 * RW_WIDTH + DECAY_LORA + A_LORA), F32))
    r, k, v, ld, a, zb, *last_p = _front_rwkv(x_prompt, mod_p, zero_shift, w, bb=1, tt=tt)
    o_a = _attn_prompt(q, keys, vals, za, tq=_tile(S, 256))
    o_b, wkv_p = _wkv(r, k, v, ld, a, zb, w, None)
    y_prompt = _merge(x_prompt, mod_p, o_a, o_b, ga, gb, w, bb=1, tt=_tile(S, 512))

    bs = _tile(Bd, 32)
    q_s, kv_s, kpe_s, kinv_s, za_s, ga_s, gb_s = _front_mla(
        x_sample, mod_s, _rope_tables(past + jnp.arange(T)), w, SCALE, bb=bs, tt=T, q_dtype=F32, emit_kv=False)
    r_s, k_s, v_s, ld_s, a_s, zb_s, *last_s = _front_rwkv(x_sample, mod_s, _split_shift(state_shift), w, bb=bs, tt=T)
    o_a_s = _attn_sample(page_table, q_s, kv_s, kpe_s, kinv_s, za_s, cache_kv, cache_kpe, cache_kinv, w,
                         pp=_tile(n_pages, 16))
    o_b_s, wkv_s = _wkv(r_s, k_s, v_s, ld_s, a_s, zb_s, w, state_wkv.reshape(Bd, RW_WIDTH, RW_HEAD))
    y_sample = _merge(x_sample, mod_s, o_a_s, o_b_s, ga_s, gb_s, w, bb=bs, tt=T)

    return (y_prompt, y_sample, kv_p, kpe_p, kinv_p, wkv_p.reshape(B, RW_HEADS, RW_HEAD, RW_HEAD),
            _join_shift(last_p), kv_s, kpe_s, kinv_s, wkv_s.reshape(Bd, RW_HEADS, RW_HEAD, RW_HEAD),
            _join_shift(last_s))
```

```python
import functools
import math

import jax
import jax.numpy as jnp
from jax import lax
from jax.experimental import pallas as pl
from jax.experimental.pallas import tpu as pltpu

F32 = jnp.float32
BF16 = jnp.bfloat16

D_MODEL = 1024
PAGE_SIZE = 128
MLA_HEADS = 16
QK_NOPE = 64
QK_ROPE = 32
QK_DIM = QK_NOPE + QK_ROPE
V_DIM = 64
Q_LORA = 384
KV_LORA = 256
ROPE_THETA = 10000.0
MLA_WIDTH = MLA_HEADS * V_DIM
SCALE = QK_DIM ** -0.5
RW_HEADS = 16
RW_HEAD = 64
RW_WIDTH = RW_HEADS * RW_HEAD
DECAY_LORA = 64
A_LORA = 64
RMS_EPS = 1e-6
GN_EPS = 64e-5

LANES = 128
HEAD_PAD = LANES
QKV_WIDTH = MLA_HEADS * HEAD_PAD
NOPE_OFF = HEAD_PAD - QK_NOPE
HALF_ROPE = QK_ROPE // 2
WKV_CHUNK = 64
WKV_GROUP = 4
GROUP_LANES = WKV_GROUP * RW_HEAD
VMEM_LIMIT = 56 * 1024 * 1024
NEG_BIG = -1e30
DECAY_SCALE = math.exp(-0.5)
LOG2_E = math.log2(math.e)
HEADS_PER_STAGE = 8
WKV_SEQS = 4
SOFTMAX_STEPS = 2
GATE_SLICE = 256


def _dot(a, b):
    return jnp.dot(a.astype(BF16), b.astype(BF16), preferred_element_type=F32)


def _dot_nt(a, b):
    return lax.dot_general(a.astype(BF16), b.astype(BF16), (((1,), (1,)), ((), ())), preferred_element_type=F32)


def _split2(x):
    hi = x.astype(BF16)
    lo = (x - hi.astype(F32)).astype(BF16)
    return hi, lo


def _split3(x):
    hi = x.astype(BF16)
    r1 = x - hi.astype(F32)
    mid = r1.astype(BF16)
    lo = (r1 - mid.astype(F32)).astype(BF16)
    return hi, mid, lo


def _sigmoid(x):
    return 1.0 / (1.0 + jnp.exp(-x))


def _silu(x):
    return x * _sigmoid(x)


def _rms(x, eps=RMS_EPS):
    return x * lax.rsqrt(jnp.mean(x * x, axis=-1, keepdims=True) + eps)


def _params(*sem):
    return pltpu.CompilerParams(dimension_semantics=sem, vmem_limit_bytes=VMEM_LIMIT)


def _const_spec(shape):
    nd = len(shape)
    return pl.BlockSpec(shape, lambda *_: (0,) * nd)


def _mod_kernel(c_ref, w_ref, b_ref, o_ref):
    o_ref[...] = _dot(_silu(c_ref[...]), w_ref[...]) + b_ref[...]


def _mod(c, w_ada, b_ada):
    n = c.shape[0]
    return pl.pallas_call(
        _mod_kernel,
        out_shape=jax.ShapeDtypeStruct((n, 3 * D_MODEL), F32),
        name="mod",
        compiler_params=pltpu.CompilerParams(vmem_limit_bytes=VMEM_LIMIT),
    )(c, w_ada, b_ada)


def _modulated(x_ref, mod_ref, gn_ref, bb, tt):
    x = x_ref[...]
    mod = mod_ref[...]
    shift = mod[:, :, :D_MODEL]
    scale = mod[:, :, D_MODEL:2 * D_MODEL]
    h = _rms(x) * gn_ref[...] * (1.0 + scale) + shift
    return h.reshape(bb * tt, D_MODEL).astype(BF16)


def _front_mla_kernel(x_ref, mod_ref, gn_ref, cos_ref, sin_ref,
                      wqa_ref, wkva_ref, wkpe_ref, wza_ref, wga_ref, wgb_ref,
                      gqa_ref, wuq_ref, wuqs_ref, gq_ref, gqs_ref, gkva_ref, wuk_ref, gkr_ref, gkrs_ref, gkn_ref,
                      wuv_ref, vone_ref, *out_refs, bb, tt, emit_kv):
    if emit_kv:
        q_ref, kv_ref, kpe_ref, kinv_ref, za_ref, ga_ref, gb_ref, k_ref, v_ref = out_refs
    else:
        q_ref, kv_ref, kpe_ref, kinv_ref, za_ref, ga_ref, gb_ref = out_refs
    tm = bb * tt
    hb = _modulated(x_ref, mod_ref, gn_ref, bb, tt)

    def table(ref):
        return jnp.broadcast_to(ref[...][None], (bb, tt, HEAD_PAD)).reshape(tm, HEAD_PAD)

    cos, sin = table(cos_ref), table(sin_ref)

    gate_slices = [(o_ref, w_ref, c) for o_ref, w_ref in ((za_ref, wza_ref), (ga_ref, wga_ref), (gb_ref, wgb_ref))
                   for c in range(0, D_MODEL, GATE_SLICE)]
    n_stages = 2 * MLA_HEADS

    def gates_due(stage):
        lo = stage * len(gate_slices) // n_stages
        hi = (stage + 1) * len(gate_slices) // n_stages
        for o_ref, w_ref, c in gate_slices[lo:hi]:
            o_ref[:, :, c:c + GATE_SLICE] = _dot(hb, w_ref[:, c:c + GATE_SLICE]).reshape(bb, tt, GATE_SLICE)

    qn = (_rms(_dot(hb, wqa_ref[...])) * gqa_ref[...]).astype(BF16)
    qf = _dot(qn, wuq_ref[...])
    qs = _dot(qn, wuqs_ref[...])
    c_lat = _rms(_dot(hb, wkva_ref[...])) * gkva_ref[...]
    kv_ref[...] = c_lat.reshape(bb, tt, KV_LORA)
    cb = c_lat.astype(BF16)
    kpe2 = _dot(hb, wkpe_ref[...])
    kpe_raw = kpe2[:, :HEAD_PAD]
    kp2 = jnp.sum(kpe_raw * kpe_raw, axis=-1, keepdims=True)
    kpe_rot = kpe_raw * (gkr_ref[...] * cos) + kpe2[:, HEAD_PAD:] * (gkrs_ref[...] * sin)
    kpe_ref[...] = kpe_rot[:, :QK_ROPE].reshape(bb, tt, QK_ROPE)
    knf = _dot(cb, wuk_ref[...])
    if emit_kv:
        v_ref[...] = (_dot(cb, wuv_ref[...]) + vone_ref[...]).reshape(bb, tt, QKV_WIDTH).astype(BF16)

    q_cos = gq_ref[...] * cos
    q_sin = gqs_ref[...] * sin
    for h in range(MLA_HEADS):
        sl = slice(h * HEAD_PAD, (h + 1) * HEAD_PAD)
        qh = qf[:, sl]
        ms = jnp.sum(qh * qh, axis=-1, keepdims=True) * (1.0 / QK_DIM)
        qh = lax.rsqrt(ms + RMS_EPS) * (qh * q_cos + qs[:, sl] * q_sin)
        q_ref[:, :, sl] = qh.reshape(bb, tt, HEAD_PAD).astype(q_ref.dtype)
        gates_due(h)

    gkn = gkn_ref[...]
    head_lane = lax.broadcasted_iota(jnp.int32, (tm, MLA_HEADS), 1)
    kinv_all = jnp.zeros((tm, MLA_HEADS), F32)
    for h in range(MLA_HEADS):
        sl = slice(h * HEAD_PAD, (h + 1) * HEAD_PAD)
        kn = knf[:, sl]
        ms = (jnp.sum(kn * kn, axis=-1, keepdims=True) + kp2) * (1.0 / QK_DIM)
        kinv = lax.rsqrt(ms + RMS_EPS)
        kinv_all = jnp.where(head_lane == h, kinv, kinv_all)
        if emit_kv:
            k_ref[:, :, sl] = ((kn * gkn + kpe_rot) * kinv).reshape(bb, tt, HEAD_PAD).astype(BF16)
        gates_due(MLA_HEADS + h)
    kinv_ref[...] = kinv_all.reshape(bb, tt, MLA_HEADS)


def _front_mla(x, mod, tabs, w, q_gain, *, bb, tt, q_dtype, emit_kv):
    nb, t_all, _ = x.shape
    grid = (nb // bb, t_all // tt)
    tok = lambda width: pl.BlockSpec((bb, tt, width), lambda i, j: (i, j, 0))
    tab = pl.BlockSpec((tt, HEAD_PAD), lambda i, j: (j, 0))
    weights = [w["w_qa"], w["w_kva"], w["w_kpe"], w["w_za"], w["w_ga"], w["w_gb"], w["g_qa"], w["w_uq"],
               w["w_uq_swap"], w["g_q"] * q_gain, w["g_q_swap"] * q_gain, w["g_kva"], w["w_uk"], w["g_k_rope"],
               w["g_k_rope_swap"], w["g_k_nope"], w["w_uv"], w["v_one"]]
    in_specs = ([tok(D_MODEL), pl.BlockSpec((bb, 1, 3 * D_MODEL), lambda i, j: (i, 0, 0)),
                 _const_spec((1, 1, D_MODEL)), tab, tab] + [_const_spec(a.shape) for a in weights])
    sds = lambda width, dt=F32: jax.ShapeDtypeStruct((nb, t_all, width), dt)
    out_shape = [sds(QKV_WIDTH, q_dtype), sds(KV_LORA), sds(QK_ROPE), sds(MLA_HEADS), sds(MLA_WIDTH), sds(D_MODEL),
                 sds(D_MODEL)]
    out_specs = [tok(QKV_WIDTH), tok(KV_LORA), tok(QK_ROPE), tok(MLA_HEADS), tok(MLA_WIDTH), tok(D_MODEL),
                 tok(D_MODEL)]
    if emit_kv:
        out_shape += [sds(QKV_WIDTH, BF16), sds(QKV_WIDTH, BF16)]
        out_specs += [tok(QKV_WIDTH), tok(QKV_WIDTH)]
    return pl.pallas_call(
        functools.partial(_front_mla_kernel, bb=bb, tt=tt, emit_kv=emit_kv),
        grid=grid, in_specs=in_specs, out_specs=out_specs, out_shape=out_shape,
        name="front_mla", compiler_params=_params("parallel", "arbitrary"),
    )(x, mod, w["g_norm"], *tabs, *weights)


def _front_rwkv_kernel(x_ref, mod_ref, gn_ref, s0r_ref, s0w_ref, s0k_ref, s0v_ref, s0a_ref,
                       mur_ref, muw_ref, muk_ref, muv_ref, mua_ref,
                       wr_ref, ww_ref, wk_ref, wv_ref, wa_ref, wzb_ref,
                       wd2_ref, w0_ref, wa2_ref, a0_ref,
                       r_ref, k_ref, v_ref, ld_ref, a_ref, zb_ref,
                       lr_ref, lw_ref, lk_ref, lv_ref, la_ref,
                       cr_ref, cw_ref, ck_ref, cv_ref, ca_ref, *, bb, tt):
    j = pl.program_id(1)
    hb = _modulated(x_ref, mod_ref, gn_ref, bb, tt)

    def shifted(w_ref, s0_ref, mu_ref, last_ref, carry_ref):
        width = w_ref.shape[1]
        p = _dot(hb, w_ref[...]).reshape(bb, tt, width)
        first = jnp.where(j == 0, s0_ref[...], carry_ref[...])
        t_idx = lax.broadcasted_iota(jnp.int32, (bb, tt, width), 1)
        prev = jnp.where(t_idx == 0, first, pltpu.roll(p, 1, 1))
        last = p[:, tt - 1:tt, :]
        last_ref[...] = last
        carry_ref[...] = last
        return (p + (prev - p) * mu_ref[...]).reshape(bb * tt, width)

    r_ref[...] = shifted(wr_ref, s0r_ref, mur_ref, lr_ref, cr_ref).reshape(bb, tt, RW_WIDTH)
    k_ref[...] = shifted(wk_ref, s0k_ref, muk_ref, lk_ref, ck_ref).reshape(bb, tt, RW_WIDTH)
    v_ref[...] = shifted(wv_ref, s0v_ref, muv_ref, lv_ref, cv_ref).reshape(bb, tt, RW_WIDTH)
    wl = shifted(ww_ref, s0w_ref, muw_ref, lw_ref, cw_ref)
    al = shifted(wa_ref, s0a_ref, mua_ref, la_ref, ca_ref)
    y = w0_ref[...] + _dot(jnp.tanh(wl), wd2_ref[...])
    ld_ref[...] = (-DECAY_SCALE * _sigmoid(y)).reshape(bb, tt, RW_WIDTH)
    a_ref[...] = _sigmoid(a0_ref[...] + _dot(al, wa2_ref[...])).reshape(bb, tt, RW_WIDTH)
    zb_ref[...] = _dot(hb, wzb_ref[...]).reshape(bb, tt, RW_WIDTH)


def _front_rwkv(x, mod, s0, w, *, bb, tt):
    nb, t_all, _ = x.shape
    grid = (nb // bb, t_all // tt)
    tok = lambda width: pl.BlockSpec((bb, tt, width), lambda i, j: (i, j, 0))
    row = lambda width: pl.BlockSpec((bb, 1, width), lambda i, j: (i, 0, 0))
    widths = (RW_WIDTH, LANES, RW_WIDTH, RW_WIDTH, LANES)
    mus = [w["mu_r"], w["mu_w"], w["mu_k"], w["mu_v"], w["mu_a"]]
    weights = [w["w_r"], w["w_wl"], w["w_k"], w["w_v"], w["w_al"], w["w_zb"], w["w_decay2"], w["w0"], w["w_a2"],
               w["a0"]]
    in_specs = ([tok(D_MODEL), row(3 * D_MODEL), _const_spec((1, 1, D_MODEL))] + [row(n) for n in widths]
                + [_const_spec(a.shape) for a in mus] + [_const_spec(a.shape) for a in weights])
    out_shape = ([jax.ShapeDtypeStruct((nb, t_all, RW_WIDTH), F32)] * 6
                 + [jax.ShapeDtypeStruct((nb, 1, n), F32) for n in widths])
    out_specs = [tok(RW_WIDTH)] * 6 + [row(n) for n in widths]
    scratch = [pltpu.VMEM((bb, 1, n), F32) for n in widths]
    return pl.pallas_call(
        functools.partial(_front_rwkv_kernel, bb=bb, tt=tt),
        grid=grid, in_specs=in_specs, out_specs=out_specs, out_shape=out_shape, scratch_shapes=scratch,
        name="front_rwkv", compiler_params=_params("parallel", "arbitrary"),
    )(x, mod, w["g_norm"], *s0, *mus, *weights)


def _attn_prompt_kernel(q_ref, k_ref, v_ref, za_ref, o_ref, m_ref, acc_ref, *, tq):
    qi = pl.program_id(1)
    row = lax.broadcasted_iota(jnp.int32, (tq, tq), 0)
    col = lax.broadcasted_iota(jnp.int32, (tq, tq), 1)
    causal = col <= row
    m_ref[...] = jnp.full(m_ref.shape, NEG_BIG, F32)
    acc_ref[...] = jnp.zeros(acc_ref.shape, F32)

    def step(j, masked):
        rows = pl.ds(pl.multiple_of(j * tq, tq), tq)
        for h0 in range(0, MLA_HEADS, HEADS_PER_STAGE):
            heads = list(range(h0, h0 + HEADS_PER_STAGE))
            sls = [slice(h * HEAD_PAD, (h + 1) * HEAD_PAD) for h in heads]
            s = [_dot_nt(q_ref[0, :, sl], k_ref[0, rows, sl]) for sl in sls]
            if masked:
                s = [jnp.where(causal, x, NEG_BIG) for x in s]
            m_old = [m_ref[h] for h in heads]
            m_new = [jnp.maximum(mo, jnp.max(x, axis=-1, keepdims=True)) for mo, x in zip(m_old, s)]
            p = [jnp.exp2(jnp.concatenate([x[:, i:i + LANES] - mn for i in range(0, tq, LANES)], axis=1))
                 for x, mn in zip(s, m_new)]
            pv = [_dot(x, v_ref[0, rows, sl]) for x, sl in zip(p, sls)]
            for h, mo, mn, x in zip(heads, m_old, m_new, pv):
                acc_ref[h] = jnp.exp2(mo - mn) * acc_ref[h] + x
                m_ref[h] = mn

    def full_block(j, carry):
        step(j, False)
        return carry

    lax.fori_loop(0, qi, full_block, 0)
    step(qi, True)

    low_half = lax.broadcasted_iota(jnp.int32, (tq, LANES), 1) < V_DIM
    for pair in range(MLA_HEADS // 2):
        even, odd = acc_ref[2 * pair], acc_ref[2 * pair + 1]
        o = jnp.where(low_half, even / even[:, V_DIM:V_DIM + 1], odd / odd[:, 0:1])
        sl = slice(pair * LANES, (pair + 1) * LANES)
        o_ref[0, :, sl] = (o * _silu(za_ref[0, :, sl])).astype(BF16)


def _attn_prompt(q, k, v, za, *, tq):
    nb, s_len, _ = q.shape
    return pl.pallas_call(
        functools.partial(_attn_prompt_kernel, tq=tq),
        grid=(nb, s_len // tq),
        in_specs=[pl.BlockSpec((1, tq, QKV_WIDTH), lambda b, i: (b, i, 0)),
                  pl.BlockSpec((1, s_len, QKV_WIDTH), lambda b, i: (b, 0, 0)),
                  pl.BlockSpec((1, s_len, QKV_WIDTH), lambda b, i: (b, 0, 0)),
                  pl.BlockSpec((1, tq, MLA_WIDTH), lambda b, i: (b, i, 0))],
        out_specs=pl.BlockSpec((1, tq, MLA_WIDTH), lambda b, i: (b, i, 0)),
        out_shape=jax.ShapeDtypeStruct((nb, s_len, MLA_WIDTH), BF16),
        scratch_shapes=[pltpu.VMEM((MLA_HEADS, tq, LANES), F32), pltpu.VMEM((MLA_HEADS, tq, HEAD_PAD), F32)],
        name="attn_prompt", compiler_params=_params("parallel", "arbitrary"),
    )(q, k, v, za)


def _attn_sample_kernel(pt_ref, q_ref, kvn_ref, kpn_ref, kin_ref, za_ref, wukt_ref, gkn_ref, wuv_ref,
                        ckv_hbm, ckpe_hbm, ckinv_hbm, o_ref,
                        kv_buf, kpe_buf, kinv_buf, sem, qlat_ref, qpe_ref, m_ref, l_ref, acc_ref, *, pp, tt):
    b, j = pl.program_id(0), pl.program_id(1)
    nb, nj = pl.num_programs(0), pl.num_programs(1)
    rows = MLA_HEADS * tt
    step = b * nj + j
    slot = lax.rem(step, 2)

    def page_copies(bb, jj, into):
        out = []
        for i in range(pp):
            page = pt_ref[bb, jj * pp + i]
            out += [pltpu.make_async_copy(ckv_hbm.at[page], kv_buf.at[into, i], sem.at[into]),
                    pltpu.make_async_copy(ckpe_hbm.at[page], kpe_buf.at[into, i], sem.at[into]),
                    pltpu.make_async_copy(ckinv_hbm.at[page], kinv_buf.at[into, i], sem.at[into])]
        return out

    @pl.when(step == 0)
    def _():
        for copy in page_copies(b, j, slot):
            copy.start()

    @pl.when(step + 1 < nb * nj)
    def _():
        wrap = j + 1 == nj
        for copy in page_copies(jnp.where(wrap, b + 1, b), jnp.where(wrap, 0, j + 1), 1 - slot):
            copy.start()

    for copy in page_copies(b, j, slot):
        copy.wait()
    pages = [buf.at[slot, i] for i in range(pp) for buf in (kv_buf, kpe_buf, kinv_buf)]

    @pl.when(j == 0)
    def _():
        gkn = gkn_ref[...]
        for h in range(MLA_HEADS):
            qh = q_ref[0, :, h * HEAD_PAD:(h + 1) * HEAD_PAD]
            qlat_ref[h * tt:(h + 1) * tt, :] = _dot(qh * gkn, wukt_ref[h])
            qpe_ref[h * tt:(h + 1) * tt, :] = qh[:, :QK_ROPE]
        m_ref[...] = jnp.full((rows, 1), NEG_BIG, F32)
        l_ref[...] = jnp.zeros((rows, 1), F32)
        acc_ref[...] = jnp.zeros((rows, KV_LORA), F32)

    qlat = qlat_ref[...].astype(BF16)
    qpe = qpe_ref[...].astype(BF16)

    def update(scores, values):
        m_old = m_ref[...]
        top = scores[0]
        for s in scores[1:]:
            top = jnp.maximum(top, s)
        m_new = jnp.maximum(m_old, jnp.max(top, axis=-1, keepdims=True))
        alpha = jnp.exp(m_old - m_new)
        ps = [jnp.exp(s - m_new) for s in scores]
        tot, pv = ps[0], _dot(ps[0], values[0])
        for p, c in zip(ps[1:], values[1:]):
            tot = tot + p
            pv = pv + _dot(p, c)
        l_ref[...] = alpha * l_ref[...] + jnp.sum(tot, axis=-1, keepdims=True)
        acc_ref[...] = alpha * acc_ref[...] + pv
        m_ref[...] = m_new

    values = [pages[3 * i][...].astype(BF16) for i in range(pp)]
    lat = [_dot_nt(qlat, cb) for cb in values]
    pe = [_dot(qpe, pages[3 * i + 1][...]) for i in range(pp)]
    scores = [((x + y).reshape(MLA_HEADS, tt, PAGE_SIZE) * pages[3 * i + 2][...][:, None, :]).reshape(rows, PAGE_SIZE)
              for i, (x, y) in enumerate(zip(lat, pe))]
    per = max(1, pp // SOFTMAX_STEPS)
    for i in range(0, pp, per):
        update(scores[i:i + per], values[i:i + per])

    @pl.when(j == pl.num_programs(1) - 1)
    def _():
        pad = lambda x: jnp.concatenate([x, jnp.zeros((PAGE_SIZE - tt, x.shape[1]), F32)], axis=0)
        key = lax.broadcasted_iota(jnp.int32, (rows, PAGE_SIZE), 1)
        tok = lax.broadcasted_iota(jnp.int32, (rows, PAGE_SIZE), 0) % tt
        expand = (lax.broadcasted_iota(jnp.int32, (rows, MLA_HEADS), 0) // tt
                  == lax.broadcasted_iota(jnp.int32, (rows, MLA_HEADS), 1)).astype(BF16)
        cn = pad(kvn_ref[0]).astype(BF16)
        k_hi, k_mid, k_lo = _split3(pad(kin_ref[0]))
        s = _dot_nt(qlat, cn) + _dot_nt(qpe, pad(kpn_ref[0]))
        s = s * (_dot_nt(expand, k_hi) + _dot_nt(expand, k_mid) + _dot_nt(expand, k_lo))
        update([jnp.where(key <= tok, s, NEG_BIG)], [cn])
        o_lat = (acc_ref[...] / l_ref[...]).astype(BF16)
        for pair in range(MLA_HEADS // 2):
            o = jnp.zeros((tt, LANES), F32)
            for h in (2 * pair, 2 * pair + 1):
                o = o + _dot(o_lat[h * tt:(h + 1) * tt, :], wuv_ref[:, h * HEAD_PAD:(h + 1) * HEAD_PAD])
            sl = slice(pair * LANES, (pair + 1) * LANES)
            o_ref[0, :, sl] = (o * _silu(za_ref[0, :, sl])).astype(BF16)


def _attn_sample(page_table, q, kv_new, kpe_new, kinv_new, za, cache_kv, cache_kpe, cache_kinv, w, *, pp):
    nb, tt, _ = q.shape
    n_pages = page_table.shape[1]
    rows = MLA_HEADS * tt
    per_b = lambda width: pl.BlockSpec((1, tt, width), lambda b, j, pt: (b, 0, 0))
    const = lambda shape: pl.BlockSpec(shape, lambda b, j, pt: (0,) * len(shape))

    kpe_t = jnp.swapaxes(cache_kpe, 1, 2)
    kinv_t = jnp.swapaxes(cache_kinv, 1, 2)
    hbm = pl.BlockSpec(memory_space=pl.ANY)
    grid_spec = pltpu.PrefetchScalarGridSpec(
        num_scalar_prefetch=1, grid=(nb, n_pages // pp),
        in_specs=[per_b(QKV_WIDTH), per_b(KV_LORA), per_b(QK_ROPE), per_b(MLA_HEADS), per_b(MLA_WIDTH),
                  const(w["w_ukt"].shape), const(w["g_k_nope"].shape), const(w["w_uv"].shape), hbm, hbm, hbm],
        out_specs=per_b(MLA_WIDTH),
        scratch_shapes=[pltpu.VMEM((2, pp, PAGE_SIZE, KV_LORA), F32), pltpu.VMEM((2, pp, QK_ROPE, PAGE_SIZE), F32),
                        pltpu.VMEM((2, pp, MLA_HEADS, PAGE_SIZE), F32), pltpu.SemaphoreType.DMA((2,)),
                        pltpu.VMEM((rows, KV_LORA), F32), pltpu.VMEM((rows, QK_ROPE), F32),
                        pltpu.VMEM((rows, 1), F32), pltpu.VMEM((rows, 1), F32), pltpu.VMEM((rows, KV_LORA), F32)])
    return pl.pallas_call(
        functools.partial(_attn_sample_kernel, pp=pp, tt=tt),
        grid_spec=grid_spec,
        out_shape=jax.ShapeDtypeStruct((nb, tt, MLA_WIDTH), BF16),
        name="attn_sample", compiler_params=_params("arbitrary", "arbitrary"),
    )(page_table, q, kv_new, kpe_new, kinv_new, za, w["w_ukt"], w["g_k_nope"], w["w_uv"], cache_kv, kpe_t, kinv_t)


def _wkv_kernel(*refs, tk, n_chunks, has_state):
    if has_state:
        (r_ref, k_ref, v_ref, ld_ref, a_ref, zb_ref, kk_ref, ka_ref, rk_ref, gg_ref, gb_ref, s0_ref,
         o_ref, so_ref, s_ref) = refs
    else:
        (r_ref, k_ref, v_ref, ld_ref, a_ref, zb_ref, kk_ref, ka_ref, rk_ref, gg_ref, gb_ref,
         o_ref, so_ref, s_ref) = refs
    C, L, G = WKV_CHUNK, GROUP_LANES, WKV_GROUP
    GC = G * C
    n_groups = RW_WIDTH // L
    ti = pl.program_id(1)
    ri = lax.broadcasted_iota(jnp.int32, (GC, L), 0)
    ci = lax.broadcasted_iota(jnp.int32, (GC, L), 1)
    blk = lambda n: (ri // n) == (ci // n)
    same_head = blk(RW_HEAD).astype(F32)
    same_head_bf = same_head.astype(BF16)
    t2 = lax.broadcasted_iota(jnp.int32, (C, GC), 0)
    s2 = lax.broadcasted_iota(jnp.int32, (C, GC), 1) % C
    strict2 = (s2 < t2).astype(F32)
    incl2 = (s2 <= t2).astype(F32)
    tri2 = jnp.concatenate([strict2, incl2], axis=0)
    blk2 = lambda n: (t2 // n) == (s2 // n)
    neg_m16 = -blk2(16).astype(F32)
    m32_off = (blk2(32) & ~blk2(16)).astype(F32)
    m64_off = (~blk2(32)).astype(F32)
    eye2 = (t2 == s2).astype(F32)
    tri = (lax.broadcasted_iota(jnp.int32, (C, C), 1) <= lax.broadcasted_iota(jnp.int32, (C, C), 0)).astype(BF16)
    ones_bd = same_head.astype(BF16)
    replicate = (lax.broadcasted_iota(jnp.int32, (RW_HEAD, L), 1) % RW_HEAD
                 == lax.broadcasted_iota(jnp.int32, (RW_HEAD, L), 0)).astype(BF16)

    def seg_sum(xs, split):
        n = xs[0].shape[0]
        if not split:
            y = _dot(jnp.concatenate(xs, axis=0), ones_bd)
            return [y[i * n:(i + 1) * n] for i in range(len(xs))]
        parts = [piece for x in xs for piece in _split2(x)]
        y = jnp.dot(jnp.concatenate(parts, axis=0), ones_bd, preferred_element_type=F32)
        return [y[2 * i * n:(2 * i + 1) * n] + y[(2 * i + 1) * n:(2 * i + 2) * n] for i in range(len(xs))]

    def expand(x):
        return jnp.concatenate([x.astype(BF16)] * G, axis=0) * same_head_bf

    stack = expand

    @pl.when(ti == 0)
    def _():
        if has_state:
            for n in range(s_ref.shape[0]):
                hi, mid, lo = _split3(s0_ref[n // n_groups, (n % n_groups) * L:(n % n_groups + 1) * L, :])
                s_ref[n] = (_dot(hi, replicate) + _dot(mid, replicate) + _dot(lo, replicate)) * same_head
        else:
            s_ref[...] = jnp.zeros(s_ref.shape, F32)

    def each(f, *cols):
        return [f(*args) for args in zip(*cols)]

    groups = list(range(n_groups))
    lanes = [slice(g * L, (g + 1) * L) for g in groups]
    cpi = 4 if n_chunks % 4 == 0 else 1

    def step(i, carry):
        if tk == C:
            rows = [(0, pl.ds(pl.multiple_of((i * cpi + j) * C, C), C)) for j in range(cpi)]
            load = lambda ref: [ref[0, rw, ln] for _, rw in rows for ln in lanes]
        else:
            rows = [(n, slice(0, tk)) for n in range(r_ref.shape[0])]
            load = lambda ref: [jnp.concatenate([ref[n, :, ln], jnp.zeros((C - tk, L), F32)], axis=0)
                                for n, _ in rows for ln in lanes]
        r, k, v, ld, a, zb = (load(ref) for ref in (r_ref, k_ref, v_ref, ld_ref, a_ref, zb_ref))
        k_k, k_a, r_k, gn_g, gn_b = ([ref[:, ln] for _ in rows for ln in lanes]
                                     for ref in (kk_ref, ka_ref, rk_ref, gg_ref, gb_ref))
        ld_split = each(_split2, ld)
        cum = each(lambda hl: _dot(tri, hl[0]) + _dot(tri, hl[1]), ld_split)
        cum_end = each(lambda x: x[C - 1:C, :], cum)
        kk = each(lambda x, y: x * y, k, k_k)
        norm = seg_sum(each(lambda x: x * x, kk), split=True)
        kk = each(lambda x, n: x * lax.rsqrt(jnp.maximum(n, 1e-24)), kk, norm)
        k = each(lambda x, a_, ka: x * (1.0 + (a_ - 1.0) * ka), k, a, k_a)
        b = each(lambda x, y: x * y, kk, a)
        grow = each(lambda x: jnp.exp(-x), cum)
        tail = each(lambda e, x: jnp.exp(e - x), cum_end, cum)
        ar = each(lambda kk_, x, l_, r_: jnp.concatenate([kk_ * jnp.exp(x - l_), r_ * jnp.exp(x)], axis=0).astype(BF16),
                  kk, cum, ld, r)
        bk = each(lambda b_, k_, g_: jnp.concatenate([stack(b_ * g_), stack(k_ * g_)], axis=0),
                  b, k, grow)
        gm = each(_dot_nt, ar, bk)
        vs = each(stack, v)
        l2b = each(lambda m: m[:C, :GC] * strict2, gm)
        kv_part = each(lambda m, vs_: _dot(m[:, GC:] * tri2, vs_), gm, vs)
        m2b = each(lambda m: (m[C:, :GC] * incl2).astype(BF16), gm)
        kb = each(lambda k_, b_, t_: jnp.concatenate([k_ * t_, b_ * t_], axis=0).astype(BF16), k, b, tail)
        decay_end = each(jnp.exp, cum_end)
        rkr = each(lambda r_, k_, rk: r_ * k_ * rk, r, k, r_k)
        p = each(lambda l_: l_ * neg_m16, l2b)
        x = each(lambda p_: eye2 + p_, p)
        p = each(lambda p_: _dot(p_, expand(p_)), p)
        for _ in range(2):
            t = each(lambda x_, p_: _dot(jnp.concatenate([x_, p_], axis=0), expand(p_)), x, p)
            x = each(lambda x_, t_: x_ + t_[:C], x, t)
            p = each(lambda t_: t_[C:], t)
        x = each(lambda x_, p_: x_ + _dot(x_, expand(p_)), x, p)
        for off in (m32_off, m64_off):
            y = each(lambda l_, x_: _dot(l_ * off, expand(x_)), l2b, x)
            x = each(lambda x_, y_: x_ - _dot(x_, expand(y_)), x, y)
        x2 = each(lambda x_: x_.astype(BF16), x)

        def state_part(part, slots):
            s_prev = [s_ref[n] for n in slots]
            ars = each(_dot_nt, ar[part], s_prev)
            rhs = each(lambda s_, kv_: s_[:C] + kv_[:C], ars, kv_part[part])
            u = each(lambda x_, rhs_: _dot(x_, stack(rhs_)), x2[part], rhs)
            out = each(lambda s_, kv_, m_, u_: s_[C:] + kv_[C:] - _dot(m_, stack(u_)), ars, kv_part[part], m2b[part], u)
            ds = each(lambda v_, u_, kb_: _dot(jnp.concatenate([v_, -u_], axis=0).T, kb_), v[part], u, kb[part])
            for n, s_, d_, ds_ in zip(slots, s_prev, decay_end[part], ds):
                s_ref[n] = s_ * d_ + ds_ * same_head
            return out

        if has_state:
            o = state_part(slice(None), list(range(len(ar))))
        else:
            o = []
            for j in range(len(rows)):
                o += state_part(slice(j * n_groups, (j + 1) * n_groups), groups)

        sums = seg_sum(o + rkr, split=False)
        mean, bonus = sums[:len(o)], sums[len(o):]
        dev = each(lambda x_, m: x_ - m * (1.0 / RW_HEAD), o, mean)
        var = seg_sum(each(lambda x_: x_ * x_, dev), split=False)
        for n, (seq, rw, ln) in enumerate((seq, rw, ln) for seq, rw in rows for ln in lanes):
            out = dev[n] * lax.rsqrt(var[n] * (1.0 / RW_HEAD) + GN_EPS) * gn_g[n] + gn_b[n] + bonus[n] * v[n]
            out = (out * _silu(zb[n])).astype(BF16)
            o_ref[seq, rw, ln] = out[:tk] if tk != C else out
        return carry

    if tk != C:
        step(0, 0)
    else:
        lax.fori_loop(0, n_chunks // cpi, step, 0)

    @pl.when(ti == pl.num_programs(1) - 1)
    def _():
        rep_t = (lax.broadcasted_iota(jnp.int32, (L, RW_HEAD), 0) % RW_HEAD
                 == lax.broadcasted_iota(jnp.int32, (L, RW_HEAD), 1)).astype(BF16)
        for n in range(s_ref.shape[0]):
            hi, mid, lo = _split3(s_ref[n])
            so_ref[n // n_groups, (n % n_groups) * L:(n % n_groups + 1) * L, :] = (
                _dot(hi, rep_t) + _dot(mid, rep_t) + _dot(lo, rep_t))


def _wkv(r, k, v, ld, a, zb, w, state):
    nb, t_all, _ = r.shape
    has_state = state is not None
    if has_state:
        tk, tb, bb = t_all, t_all, _tile(nb, WKV_SEQS)
    else:
        tk, tb, bb = WKV_CHUNK, _tile(t_all, 512), 1
    n_groups = RW_WIDTH // GROUP_LANES
    seq = pl.BlockSpec((bb, tb, RW_WIDTH), lambda b, t: (b, t, 0))
    vec = pl.BlockSpec((1, RW_WIDTH), lambda b, t: (0, 0))
    st = pl.BlockSpec((bb, RW_WIDTH, RW_HEAD), lambda b, t: (b, 0, 0))
    in_specs = [seq] * 6 + [vec] * 5 + ([st] if has_state else [])
    args = [r, k, v, ld, a, zb, w["k_k"], w["k_a"], w["r_k"], w["gn_g"], w["gn_b"]] + ([state] if has_state else [])
    return pl.pallas_call(
        functools.partial(_wkv_kernel, tk=tk, n_chunks=tb // tk, has_state=has_state),
        grid=(nb // bb, t_all // tb), in_specs=in_specs, out_specs=[seq, st],
        out_shape=[jax.ShapeDtypeStruct((nb, t_all, RW_WIDTH), BF16),
                   jax.ShapeDtypeStruct((nb, RW_WIDTH, RW_HEAD), F32)],
        scratch_shapes=[pltpu.VMEM((bb * n_groups, GROUP_LANES, GROUP_LANES), F32)],
        name="wkv", compiler_params=_params("parallel", "arbitrary"),
    )(*args)


def _merge_kernel(x_ref, mod_ref, oa_ref, ob_ref, ga_ref, gb_ref, wpa_ref, wpb_ref, wout_ref, y_ref, *, bb, tt):
    tm = bb * tt
    flat = lambda ref: ref[...].reshape(tm, ref.shape[2])
    m = (_sigmoid(flat(ga_ref)) * _dot(flat(oa_ref), wpa_ref[...])
         + _sigmoid(flat(gb_ref)) * _dot(flat(ob_ref), wpb_ref[...]))
    gate = mod_ref[...][:, :, 2 * D_MODEL:]
    y_ref[...] = x_ref[...] + gate * _dot(m, wout_ref[...]).reshape(bb, tt, D_MODEL)


def _merge(x, mod, oa, ob, ga, gb, w, *, bb, tt):
    nb, t_all, _ = x.shape
    tok = pl.BlockSpec((bb, tt, D_MODEL), lambda i, j: (i, j, 0))
    return pl.pallas_call(
        functools.partial(_merge_kernel, bb=bb, tt=tt),
        grid=(nb // bb, t_all // tt),
        in_specs=[tok, pl.BlockSpec((bb, 1, 3 * D_MODEL), lambda i, j: (i, 0, 0)), tok, tok, tok, tok,
                  _const_spec(w["w_pa"].shape), _const_spec(w["w_pb"].shape), _const_spec(w["w_out"].shape)],
        out_specs=tok, out_shape=jax.ShapeDtypeStruct(x.shape, F32),
        name="merge", compiler_params=_params("parallel", "parallel"),
    )(x, mod, oa, ob, ga, gb, w["w_pa"], w["w_pb"], w["w_out"])


def _head_slabs(rope_part, nope_part):
    lead = nope_part.shape[:-1]
    parts = []
    if rope_part is None:
        parts.append(jnp.zeros(lead + (NOPE_OFF,), nope_part.dtype))
    else:
        parts += [rope_part, jnp.zeros(lead + (NOPE_OFF - QK_ROPE,), nope_part.dtype)]
    parts.append(nope_part)
    slab = jnp.concatenate(parts, axis=-1)
    return slab.reshape(lead[:-1] + (MLA_HEADS * HEAD_PAD,))


def _pad_cols(x, width):
    return jnp.pad(x, ((0, 0), (0, width - x.shape[1])))


def _prepare_weights(w_ada, b_ada, g_norm, w_in, g_qa, w_uq, g_kva, w_uk, w_uv, g_q, g_k, w_pa, mu_shift, w0,
                     w_decay2, a0, w_a2, k_k, k_a, r_k, gn_g, gn_b, w_pb, w_out):
    bf = lambda x: x.astype(BF16)
    row = lambda x: x.reshape(1, -1)
    splits = (Q_LORA, KV_LORA, QK_ROPE, MLA_WIDTH, RW_WIDTH, DECAY_LORA, RW_WIDTH, RW_WIDTH, A_LORA, RW_WIDTH,
              D_MODEL, D_MODEL)
    offs = [0]
    for s in splits:
        offs.append(offs[-1] + s)
    cols = [w_in[:, offs[i]:offs[i + 1]] for i in range(len(splits))]
    w_qa, w_kva, w_kpe, w_za, w_r, w_wl, w_k, w_v, w_al, w_zb, w_ga, w_gb = cols
    mu = [mu_shift[offs[i] - offs[4]:offs[i + 1] - offs[4]] for i in range(4, 9)]
    pad_row = lambda x: _pad_cols(row(x), LANES).reshape(1, 1, LANES)
    row3 = lambda x: x.reshape(1, 1, -1)
    zeros_h = jnp.zeros((KV_LORA, MLA_HEADS, V_DIM), F32)
    parity = (jnp.arange(MLA_HEADS) % 2 == 0)[None, :, None]
    w_uv_slab = jnp.concatenate([jnp.where(parity, w_uv, zeros_h), jnp.where(parity, zeros_h, w_uv)], axis=-1)
    w_ukt = jnp.pad(jnp.transpose(w_uk, (1, 2, 0)), ((0, 0), (NOPE_OFF, 0), (0, 0)))
    swap_cols = lambda x: jnp.concatenate([-x[..., HALF_ROPE:], x[..., :HALF_ROPE]], axis=-1)
    swap_gain = lambda g: jnp.pad(jnp.concatenate([g[HALF_ROPE:], g[:HALF_ROPE]]), (0, HEAD_PAD - QK_ROPE))
    w_uq_swap = _head_slabs(swap_cols(w_uq[..., QK_NOPE:]), jnp.zeros_like(w_uq[..., :QK_NOPE]))
    w_kpe_both = jnp.concatenate([_pad_cols(w_kpe, HEAD_PAD), _pad_cols(swap_cols(w_kpe), HEAD_PAD)], axis=1)
    g_k_rope = jnp.pad(g_k[QK_NOPE:], (0, HEAD_PAD - QK_ROPE))
    g_k_nope = jnp.pad(g_k[:QK_NOPE], (NOPE_OFF, 0))
    g_q_slab = jnp.concatenate([g_q[QK_NOPE:], jnp.zeros((NOPE_OFF - QK_ROPE,), F32), g_q[:QK_NOPE]])
    lane = jnp.arange(QKV_WIDTH) % HEAD_PAD
    even = (jnp.arange(QKV_WIDTH) // HEAD_PAD) % 2 == 0
    v_one = jnp.where(even, lane == V_DIM, lane == 0).astype(F32)
    return {
        "w_ada": bf(w_ada), "b_ada": row(b_ada), "g_norm": row3(g_norm),
        "w_qa": bf(w_qa), "w_kva": bf(w_kva), "w_kpe": bf(w_kpe_both), "w_za": bf(w_za),
        "w_ga": bf(w_ga), "w_gb": bf(w_gb), "g_qa": row(g_qa), "g_kva": row(g_kva),
        "w_uq": bf(_head_slabs(w_uq[..., QK_NOPE:], w_uq[..., :QK_NOPE])),
        "w_uk": bf(_head_slabs(None, w_uk)), "w_uv": bf(w_uv_slab.reshape(KV_LORA, QKV_WIDTH)),
        "w_uq_swap": bf(w_uq_swap), "w_ukt": bf(w_ukt), "g_q": row(g_q_slab),
        "g_q_swap": row(swap_gain(g_q[QK_NOPE:])), "v_one": row(v_one),
        "g_k_rope": row(g_k_rope), "g_k_rope_swap": row(swap_gain(g_k[QK_NOPE:])), "g_k_nope": row(g_k_nope),
        "w_r": bf(w_r), "w_wl": bf(_pad_cols(w_wl, LANES)), "w_k": bf(w_k), "w_v": bf(w_v),
        "w_al": bf(_pad_cols(w_al, LANES)), "w_zb": bf(w_zb),
        "mu_r": row3(mu[0]), "mu_w": pad_row(mu[1]), "mu_k": row3(mu[2]), "mu_v": row3(mu[3]), "mu_a": pad_row(mu[4]),
        "w_decay2": bf(jnp.pad(w_decay2, ((0, LANES - DECAY_LORA), (0, 0)))), "w0": row(w0),
        "w_a2": bf(jnp.pad(w_a2, ((0, LANES - A_LORA), (0, 0)))), "a0": row(a0),
        "k_k": row(k_k), "k_a": row(k_a), "r_k": row(r_k), "gn_g": row(gn_g), "gn_b": row(gn_b),
        "w_pa": bf(w_pa), "w_pb": bf(w_pb), "w_out": bf(w_out),
    }


def _rope_tables(pos):
    inv = ROPE_THETA ** (-jnp.arange(HALF_ROPE, dtype=F32) / HALF_ROPE)
    ang = pos.astype(F32)[:, None] * inv[None, :]
    cos, sin = jnp.cos(ang), jnp.sin(ang)
    n = pos.shape[0]
    rest = jnp.zeros((n, HEAD_PAD - QK_ROPE), F32)
    return jnp.concatenate([cos, cos, rest + 1.0], axis=1), jnp.concatenate([sin, sin, rest], axis=1)


def _split_shift(state_shift):
    o1 = RW_WIDTH
    o2 = o1 + DECAY_LORA
    o3 = o2 + RW_WIDTH
    o4 = o3 + RW_WIDTH
    n = state_shift.shape[0]
    pad = lambda x: _pad_cols(x, LANES)
    parts = (state_shift[:, :o1], pad(state_shift[:, o1:o2]), state_shift[:, o2:o3], state_shift[:, o3:o4],
             pad(state_shift[:, o4:]))
    return [p.reshape(n, 1, -1) for p in parts]


def _join_shift(last):
    lr, lw, lk, lv, la = (x[:, 0, :] for x in last)
    return jnp.concatenate([lr, lw[:, :DECAY_LORA], lk, lv, la[:, :A_LORA]], axis=1)


def _tile(n, target):
    t = min(n, target)
    while n % t:
        t //= 2
    return t


def kernel(x_prompt, x_sample, c_prompt, c_sample, cache_kv, cache_kpe, cache_kinv, state_wkv, state_shift, page_table, w_ada, b_ada, g_norm, w_in, g_qa, w_uq, g_kva, w_uk, w_uv, g_q, g_k, w_pa, mu_shift, w0, w_decay2, a0, w_a2, k_k, k_a, r_k, gn_g, gn_b, w_pb, w_out):
    B, S, _ = x_prompt.shape
    Bd, T, _ = x_sample.shape
    n_pages = page_table.shape[1]
    past = n_pages * PAGE_SIZE
    w = _prepare_weights(w_ada, b_ada, g_norm, w_in, g_qa, w_uq, g_kva, w_uk, w_uv, g_q, g_k, w_pa, mu_shift, w0,
                         w_decay2, a0, w_a2, k_k, k_a, r_k, gn_g, gn_b, w_pb, w_out)

    mod = _mod(jnp.concatenate([c_prompt, c_sample], axis=0), w["w_ada"], w["b_ada"])
    mod_p = mod[:B].reshape(B, 1, 3 * D_MODEL)
    mod_s = mod[B:].reshape(Bd, 1, 3 * D_MODEL)

    tt = _tile(S, 256)
    q, kv_p, kpe_p, kinv_p, za, ga, gb, keys, vals = _front_mla(
        x_prompt, mod_p, _rope_tables(jnp.arange(S)), w, SCALE * LOG2_E, bb=1, tt=tt, q_dtype=BF16, emit_kv=True)
    zero_shift = _split_shift(jnp.zeros((B, 3 * RW_WIDTH + DECAY_LORA + A_LORA), F32))
    r, k, v, ld, a, zb, *last_p = _front_rwkv(x_prompt, mod_p, zero_shift, w, bb=1, tt=_tile(S, 512))
    o_a = _attn_prompt(q, keys, vals, za, tq=_tile(S, 256))
    o_b, wkv_p = _wkv(r, k, v, ld, a, zb, w, None)
    y_prompt = _merge(x_prompt, mod_p, o_a, o_b, ga, gb, w, bb=1, tt=_tile(S, 512))

    bs = _tile(Bd, 32)
    q_s, kv_s, kpe_s, kinv_s, za_s, ga_s, gb_s = _front_mla(
        x_sample, mod_s, _rope_tables(past + jnp.arange(T)), w, SCALE, bb=bs, tt=T, q_dtype=F32, emit_kv=False)
    r_s, k_s, v_s, ld_s, a_s, zb_s, *last_s = _front_rwkv(x_sample, mod_s, _split_shift(state_shift), w, bb=bs, tt=T)
    o_a_s = _attn_sample(page_table, q_s, kv_s, kpe_s, kinv_s, za_s, cache_kv, cache_kpe, cache_kinv, w,
                         pp=_tile(n_pages, 16))
    o_b_s, wkv_s = _wkv(r_s, k_s, v_s, ld_s, a_s, zb_s, w, state_wkv.reshape(Bd, RW_WIDTH, RW_HEAD))
    y_sample = _merge(x_sample, mod_s, o_a_s, o_b_s, ga_s, gb_s, w, bb=bs, tt=T)

    return (y_prompt, y_sample, kv_p, kpe_p, kinv_p, wkv_p.reshape(B, RW_HEADS, RW_HEAD, RW_HEAD),
            _join_shift(last_p), kv_s, kpe_s, kinv_s, wkv_s.reshape(Bd, RW_HEADS, RW_HEAD, RW_HEAD),
            _join_shift(last_s))
```

```python
import functools
import math

import jax
import jax.numpy as jnp
from jax import lax
from jax.experimental import pallas as pl
from jax.experimental.pallas import tpu as pltpu

F32 = jnp.float32
BF16 = jnp.bfloat16

D_MODEL = 1024
PAGE_SIZE = 128
MLA_HEADS = 16
QK_NOPE = 64
QK_ROPE = 32
QK_DIM = QK_NOPE + QK_ROPE
V_DIM = 64
Q_LORA = 384
KV_LORA = 256
ROPE_THETA = 10000.0
MLA_WIDTH = MLA_HEADS * V_DIM
SCALE = QK_DIM ** -0.5
RW_HEADS = 16
RW_HEAD = 64
RW_WIDTH = RW_HEADS * RW_HEAD
DECAY_LORA = 64
A_LORA = 64
RMS_EPS = 1e-6
GN_EPS = 64e-5

LANES = 128
HEAD_PAD = LANES
QKV_WIDTH = MLA_HEADS * HEAD_PAD
NOPE_OFF = HEAD_PAD - QK_NOPE
HALF_ROPE = QK_ROPE // 2
WKV_CHUNK = 64
WKV_GROUP = 4
GROUP_LANES = WKV_GROUP * RW_HEAD
VMEM_LIMIT = 56 * 1024 * 1024
NEG_BIG = -1e30
DECAY_SCALE = math.exp(-0.5)
LOG2_E = math.log2(math.e)
HEADS_PER_STAGE = 16
ACC_ROWS = V_DIM + 16
WKV_SEQS = 4
SOFTMAX_STEPS = 2
GATE_SLICE = 256


def _dot(a, b):
    return jnp.dot(a.astype(BF16), b.astype(BF16), preferred_element_type=F32)


def _dot_nt(a, b):
    return lax.dot_general(a.astype(BF16), b.astype(BF16), (((1,), (1,)), ((), ())), preferred_element_type=F32)


def _split2(x):
    hi = x.astype(BF16)
    lo = (x - hi.astype(F32)).astype(BF16)
    return hi, lo


def _split3(x):
    hi = x.astype(BF16)
    r1 = x - hi.astype(F32)
    mid = r1.astype(BF16)
    lo = (r1 - mid.astype(F32)).astype(BF16)
    return hi, mid, lo


def _sigmoid(x):
    return 1.0 / (1.0 + jnp.exp(-x))


def _silu(x):
    return x * _sigmoid(x)


def _rms(x, eps=RMS_EPS):
    return x * lax.rsqrt(jnp.mean(x * x, axis=-1, keepdims=True) + eps)


def _params(*sem):
    return pltpu.CompilerParams(dimension_semantics=sem, vmem_limit_bytes=VMEM_LIMIT)


def _const_spec(shape):
    nd = len(shape)
    return pl.BlockSpec(shape, lambda *_: (0,) * nd)


def _mod_kernel(c_ref, w_ref, b_ref, o_ref):
    o_ref[...] = _dot(_silu(c_ref[...]), w_ref[...]) + b_ref[...]


def _mod(c, w_ada, b_ada):
    n = c.shape[0]
    return pl.pallas_call(
        _mod_kernel,
        out_shape=jax.ShapeDtypeStruct((n, 3 * D_MODEL), F32),
        name="mod",
        compiler_params=pltpu.CompilerParams(vmem_limit_bytes=VMEM_LIMIT),
    )(c, w_ada, b_ada)


def _modulated(x_ref, mod_ref, gn_ref, bb, tt):
    x = x_ref[...]
    mod = mod_ref[...]
    shift = mod[:, :, :D_MODEL]
    scale = mod[:, :, D_MODEL:2 * D_MODEL]
    h = _rms(x) * gn_ref[...] * (1.0 + scale) + shift
    return h.reshape(bb * tt, D_MODEL).astype(BF16)


def _front_mla_kernel(x_ref, mod_ref, gn_ref, cos_ref, sin_ref,
                      wqa_ref, wkva_ref, wkpe_ref, wza_ref, wga_ref, wgb_ref,
                      gqa_ref, wuq_ref, wuqs_ref, gq_ref, gqs_ref, gkva_ref, wuk_ref, gkr_ref, gkrs_ref, gkn_ref,
                      wuvt_ref, *out_refs, bb, tt, emit_kv):
    if emit_kv:
        q_ref, kv_ref, kpe_ref, kinv_ref, za_ref, ga_ref, gb_ref, k_ref, vt_ref = out_refs
    else:
        q_ref, kv_ref, kpe_ref, kinv_ref, za_ref, ga_ref, gb_ref = out_refs
    tm = bb * tt
    hb = _modulated(x_ref, mod_ref, gn_ref, bb, tt)

    def table(ref):
        return jnp.broadcast_to(ref[...][None], (bb, tt, HEAD_PAD)).reshape(tm, HEAD_PAD)

    cos, sin = table(cos_ref), table(sin_ref)

    gate_slices = [(o_ref, w_ref, c) for o_ref, w_ref in ((za_ref, wza_ref), (ga_ref, wga_ref), (gb_ref, wgb_ref))
                   for c in range(0, D_MODEL, GATE_SLICE)]
    n_stages = 2 * MLA_HEADS

    def gates_due(stage):
        lo = stage * len(gate_slices) // n_stages
        hi = (stage + 1) * len(gate_slices) // n_stages
        for o_ref, w_ref, c in gate_slices[lo:hi]:
            o_ref[:, :, c:c + GATE_SLICE] = _dot(hb, w_ref[:, c:c + GATE_SLICE]).reshape(bb, tt, GATE_SLICE)

    qn = (_rms(_dot(hb, wqa_ref[...])) * gqa_ref[...]).astype(BF16)
    qf = _dot(qn, wuq_ref[...])
    qs = _dot(qn, wuqs_ref[...])
    c_lat = _rms(_dot(hb, wkva_ref[...])) * gkva_ref[...]
    kv_ref[...] = c_lat.reshape(bb, tt, KV_LORA)
    cb = c_lat.astype(BF16)
    kpe2 = _dot(hb, wkpe_ref[...])
    kpe_raw = kpe2[:, :HEAD_PAD]
    kp2 = jnp.sum(kpe_raw * kpe_raw, axis=-1, keepdims=True)
    kpe_rot = kpe_raw * (gkr_ref[...] * cos) + kpe2[:, HEAD_PAD:] * (gkrs_ref[...] * sin)
    kpe_ref[...] = kpe_rot[:, :QK_ROPE].reshape(bb, tt, QK_ROPE)
    knf = _dot(cb, wuk_ref[...])
    if emit_kv:
        vt_ref[0, 0] = _dot_nt(wuvt_ref[...], cb).astype(BF16)

    q_cos = gq_ref[...] * cos
    q_sin = gqs_ref[...] * sin
    for h in range(MLA_HEADS):
        sl = slice(h * HEAD_PAD, (h + 1) * HEAD_PAD)
        qh = qf[:, sl]
        ms = jnp.sum(qh * qh, axis=-1, keepdims=True) * (1.0 / QK_DIM)
        qh = lax.rsqrt(ms + RMS_EPS) * (qh * q_cos + qs[:, sl] * q_sin)
        q_ref[:, :, sl] = qh.reshape(bb, tt, HEAD_PAD).astype(q_ref.dtype)
        gates_due(h)

    gkn = gkn_ref[...]
    head_lane = lax.broadcasted_iota(jnp.int32, (tm, MLA_HEADS), 1)
    kinv_all = jnp.zeros((tm, MLA_HEADS), F32)
    for h in range(MLA_HEADS):
        sl = slice(h * HEAD_PAD, (h + 1) * HEAD_PAD)
        kn = knf[:, sl]
        ms = (jnp.sum(kn * kn, axis=-1, keepdims=True) + kp2) * (1.0 / QK_DIM)
        kinv = lax.rsqrt(ms + RMS_EPS)
        kinv_all = jnp.where(head_lane == h, kinv, kinv_all)
        if emit_kv:
            k_ref[:, :, sl] = ((kn * gkn + kpe_rot) * kinv).reshape(bb, tt, HEAD_PAD).astype(BF16)
        gates_due(MLA_HEADS + h)
    kinv_ref[...] = kinv_all.reshape(bb, tt, MLA_HEADS)


def _front_mla(x, mod, tabs, w, q_gain, *, bb, tt, q_dtype, emit_kv):
    nb, t_all, _ = x.shape
    grid = (nb // bb, t_all // tt)
    tok = lambda width: pl.BlockSpec((bb, tt, width), lambda i, j: (i, j, 0))
    tab = pl.BlockSpec((tt, HEAD_PAD), lambda i, j: (j, 0))
    weights = [w["w_qa"], w["w_kva"], w["w_kpe"], w["w_za"], w["w_ga"], w["w_gb"], w["g_qa"], w["w_uq"],
               w["w_uq_swap"], w["g_q"] * q_gain, w["g_q_swap"] * q_gain, w["g_kva"], w["w_uk"], w["g_k_rope"],
               w["g_k_rope_swap"], w["g_k_nope"], w["w_uv_t"]]
    in_specs = ([tok(D_MODEL), pl.BlockSpec((bb, 1, 3 * D_MODEL), lambda i, j: (i, 0, 0)),
                 _const_spec((1, 1, D_MODEL)), tab, tab] + [_const_spec(a.shape) for a in weights])
    sds = lambda width, dt=F32: jax.ShapeDtypeStruct((nb, t_all, width), dt)
    out_shape = [sds(QKV_WIDTH, q_dtype), sds(KV_LORA), sds(QK_ROPE), sds(MLA_HEADS), sds(MLA_WIDTH), sds(D_MODEL),
                 sds(D_MODEL)]
    out_specs = [tok(QKV_WIDTH), tok(KV_LORA), tok(QK_ROPE), tok(MLA_HEADS), tok(MLA_WIDTH), tok(D_MODEL),
                 tok(D_MODEL)]
    if emit_kv:
        assert bb == 1
        out_shape += [sds(QKV_WIDTH, BF16), jax.ShapeDtypeStruct((nb, t_all // tt, MLA_WIDTH, tt), BF16)]
        out_specs += [tok(QKV_WIDTH), pl.BlockSpec((1, 1, MLA_WIDTH, tt), lambda i, j: (i, j, 0, 0))]
    return pl.pallas_call(
        functools.partial(_front_mla_kernel, bb=bb, tt=tt, emit_kv=emit_kv),
        grid=grid, in_specs=in_specs, out_specs=out_specs, out_shape=out_shape,
        name="front_mla", compiler_params=_params("parallel", "arbitrary"),
    )(x, mod, w["g_norm"], *tabs, *weights)


def _front_rwkv_kernel(x_ref, mod_ref, gn_ref, s0r_ref, s0w_ref, s0k_ref, s0v_ref, s0a_ref,
                       mur_ref, muw_ref, muk_ref, muv_ref, mua_ref,
                       wr_ref, ww_ref, wk_ref, wv_ref, wa_ref, wzb_ref,
                       wd2_ref, w0_ref, wa2_ref, a0_ref,
                       r_ref, k_ref, v_ref, ld_ref, a_ref, zb_ref,
                       lr_ref, lw_ref, lk_ref, lv_ref, la_ref,
                       cr_ref, cw_ref, ck_ref, cv_ref, ca_ref, *, bb, tt):
    j = pl.program_id(1)
    hb = _modulated(x_ref, mod_ref, gn_ref, bb, tt)

    def shifted(w_ref, s0_ref, mu_ref, last_ref, carry_ref):
        width = w_ref.shape[1]
        p = _dot(hb, w_ref[...]).reshape(bb, tt, width)
        first = jnp.where(j == 0, s0_ref[...], carry_ref[...])
        t_idx = lax.broadcasted_iota(jnp.int32, (bb, tt, width), 1)
        prev = jnp.where(t_idx == 0, first, pltpu.roll(p, 1, 1))
        last = p[:, tt - 1:tt, :]
        last_ref[...] = last
        carry_ref[...] = last
        return (p + (prev - p) * mu_ref[...]).reshape(bb * tt, width)

    r_ref[...] = shifted(wr_ref, s0r_ref, mur_ref, lr_ref, cr_ref).reshape(bb, tt, RW_WIDTH)
    k_ref[...] = shifted(wk_ref, s0k_ref, muk_ref, lk_ref, ck_ref).reshape(bb, tt, RW_WIDTH)
    v_ref[...] = shifted(wv_ref, s0v_ref, muv_ref, lv_ref, cv_ref).reshape(bb, tt, RW_WIDTH)
    wl = shifted(ww_ref, s0w_ref, muw_ref, lw_ref, cw_ref)
    al = shifted(wa_ref, s0a_ref, mua_ref, la_ref, ca_ref)
    y = w0_ref[...] + _dot(jnp.tanh(wl), wd2_ref[...])
    ld_ref[...] = (-DECAY_SCALE * _sigmoid(y)).reshape(bb, tt, RW_WIDTH)
    a_ref[...] = _sigmoid(a0_ref[...] + _dot(al, wa2_ref[...])).reshape(bb, tt, RW_WIDTH)
    zb_ref[...] = _dot(hb, wzb_ref[...]).reshape(bb, tt, RW_WIDTH)


def _front_rwkv(x, mod, s0, w, *, bb, tt):
    nb, t_all, _ = x.shape
    grid = (nb // bb, t_all // tt)
    tok = lambda width: pl.BlockSpec((bb, tt, width), lambda i, j: (i, j, 0))
    row = lambda width: pl.BlockSpec((bb, 1, width), lambda i, j: (i, 0, 0))
    widths = (RW_WIDTH, LANES, RW_WIDTH, RW_WIDTH, LANES)
    mus = [w["mu_r"], w["mu_w"], w["mu_k"], w["mu_v"], w["mu_a"]]
    weights = [w["w_r"], w["w_wl"], w["w_k"], w["w_v"], w["w_al"], w["w_zb"], w["w_decay2"], w["w0"], w["w_a2"],
               w["a0"]]
    in_specs = ([tok(D_MODEL), row(3 * D_MODEL), _const_spec((1, 1, D_MODEL))] + [row(n) for n in widths]
                + [_const_spec(a.shape) for a in mus] + [_const_spec(a.shape) for a in weights])
    out_shape = ([jax.ShapeDtypeStruct((nb, t_all, RW_WIDTH), F32)] * 6
                 + [jax.ShapeDtypeStruct((nb, 1, n), F32) for n in widths])
    out_specs = [tok(RW_WIDTH)] * 6 + [row(n) for n in widths]
    scratch = [pltpu.VMEM((bb, 1, n), F32) for n in widths]
    return pl.pallas_call(
        functools.partial(_front_rwkv_kernel, bb=bb, tt=tt),
        grid=grid, in_specs=in_specs, out_specs=out_specs, out_shape=out_shape, scratch_shapes=scratch,
        name="front_rwkv", compiler_params=_params("parallel", "arbitrary"),
    )(x, mod, w["g_norm"], *s0, *mus, *weights)


def _attn_prompt_kernel(q_ref, k_ref, vt_ref, za_ref, o_ref, qt_ref, m_ref, acc_ref, *, tq, tk):
    qi = pl.program_id(1)
    n_diag = tq // tk
    n_full = qi * n_diag
    key = lax.broadcasted_iota(jnp.int32, (tk, tq), 0)
    qry = lax.broadcasted_iota(jnp.int32, (tk, tq), 1)
    for h in range(MLA_HEADS):
        qt_ref[h] = q_ref[0, :, h * HEAD_PAD:(h + 1) * HEAD_PAD].astype(F32).T.astype(BF16)
    m_ref[...] = jnp.full(m_ref.shape, NEG_BIG, F32)
    ones_rows = jnp.ones((ACC_ROWS - V_DIM, tk), BF16)
    acc_ref[...] = jnp.zeros(acc_ref.shape, F32)

    def step(j, diag):
        rows = pl.ds(pl.multiple_of(j * tk, tk), tk)
        allowed = None if diag is None else key + diag * tk <= qry
        for h0 in range(0, MLA_HEADS, HEADS_PER_STAGE):
            heads = list(range(h0, h0 + HEADS_PER_STAGE))
            s = [_dot(k_ref[0, rows, h * HEAD_PAD:(h + 1) * HEAD_PAD], qt_ref[h]) for h in heads]
            if allowed is not None:
                s = [jnp.where(allowed, x, NEG_BIG) for x in s]
            m_old = [m_ref[h] for h in heads]
            m_new = [jnp.maximum(mo, jnp.max(x, axis=0, keepdims=True)) for mo, x in zip(m_old, s)]
            p = [jnp.exp2(x - mn) for x, mn in zip(s, m_new)]
            pv = [_dot(jnp.concatenate([vt_ref[0, j, h * V_DIM:(h + 1) * V_DIM, :], ones_rows], axis=0), x)
                  for h, x in zip(heads, p)]
            for h, mo, mn, y in zip(heads, m_old, m_new, pv):
                acc_ref[h] = jnp.exp2(mo - mn) * acc_ref[h] + y
                m_ref[h] = mn

    def full_block(j, carry):
        step(j, None)
        return carry

    lax.fori_loop(0, n_full, full_block, 0)
    for d in range(n_diag):
        step(n_full + d, d)

    for pair in range(MLA_HEADS // 2):
        o_t = jnp.concatenate([acc_ref[h, :V_DIM, :] / acc_ref[h, V_DIM:V_DIM + 1, :]
                               for h in (2 * pair, 2 * pair + 1)], axis=0)
        sl = slice(pair * LANES, (pair + 1) * LANES)
        o_ref[0, :, sl] = (o_t.T * _silu(za_ref[0, :, sl])).astype(BF16)


def _attn_prompt(q, k, vt, za, *, tq, tk):
    nb, s_len, _ = q.shape
    return pl.pallas_call(
        functools.partial(_attn_prompt_kernel, tq=tq, tk=tk),
        grid=(nb, s_len // tq),
        in_specs=[pl.BlockSpec((1, tq, QKV_WIDTH), lambda b, i: (b, i, 0)),
                  pl.BlockSpec((1, s_len, QKV_WIDTH), lambda b, i: (b, 0, 0)),
                  pl.BlockSpec((1, s_len // tk, MLA_WIDTH, tk), lambda b, i: (b, 0, 0, 0)),
                  pl.BlockSpec((1, tq, MLA_WIDTH), lambda b, i: (b, i, 0))],
        out_specs=pl.BlockSpec((1, tq, MLA_WIDTH), lambda b, i: (b, i, 0)),
        out_shape=jax.ShapeDtypeStruct((nb, s_len, MLA_WIDTH), BF16),
        scratch_shapes=[pltpu.VMEM((MLA_HEADS, HEAD_PAD, tq), BF16), pltpu.VMEM((MLA_HEADS, 1, tq), F32),
                        pltpu.VMEM((MLA_HEADS, ACC_ROWS, tq), F32)],
        name="attn_prompt", compiler_params=_params("parallel", "arbitrary"),
    )(q, k, vt, za)


def _attn_sample_kernel(pt_ref, q_ref, kvn_ref, kpn_ref, kin_ref, za_ref, wukt_ref, gkn_ref, wuv_ref,
                        ckv_hbm, ckpe_hbm, ckinv_hbm, o_ref,
                        kv_buf, kpe_buf, kinv_buf, sem, qlat_ref, qpe_ref, m_ref, l_ref, acc_ref, *, pp, tt):
    b, j = pl.program_id(0), pl.program_id(1)
    nb, nj = pl.num_programs(0), pl.num_programs(1)
    rows = MLA_HEADS * tt
    step = b * nj + j
    slot = lax.rem(step, 2)

    def page_copies(bb, jj, into):
        out = []
        for i in range(pp):
            page = pt_ref[bb, jj * pp + i]
            out += [pltpu.make_async_copy(ckv_hbm.at[page], kv_buf.at[into, i], sem.at[into]),
                    pltpu.make_async_copy(ckpe_hbm.at[page], kpe_buf.at[into, i], sem.at[into]),
                    pltpu.make_async_copy(ckinv_hbm.at[page], kinv_buf.at[into, i], sem.at[into])]
        return out

    @pl.when(step == 0)
    def _():
        for copy in page_copies(b, j, slot):
            copy.start()

    @pl.when(step + 1 < nb * nj)
    def _():
        wrap = j + 1 == nj
        for copy in page_copies(jnp.where(wrap, b + 1, b), jnp.where(wrap, 0, j + 1), 1 - slot):
            copy.start()

    for copy in page_copies(b, j, slot):
        copy.wait()
    pages = [buf.at[slot, i] for i in range(pp) for buf in (kv_buf, kpe_buf, kinv_buf)]

    @pl.when(j == 0)
    def _():
        gkn = gkn_ref[...]
        for h in range(MLA_HEADS):
            qh = q_ref[0, :, h * HEAD_PAD:(h + 1) * HEAD_PAD]
            qlat_ref[h * tt:(h + 1) * tt, :] = _dot(qh * gkn, wukt_ref[h])
            qpe_ref[h * tt:(h + 1) * tt, :] = qh[:, :QK_ROPE]
        m_ref[...] = jnp.full((rows, 1), NEG_BIG, F32)
        l_ref[...] = jnp.zeros((rows, 1), F32)
        acc_ref[...] = jnp.zeros((rows, KV_LORA), F32)

    qlat = qlat_ref[...].astype(BF16)
    qpe = qpe_ref[...].astype(BF16)

    def update(scores, values):
        m_old = m_ref[...]
        top = scores[0]
        for s in scores[1:]:
            top = jnp.maximum(top, s)
        m_new = jnp.maximum(m_old, jnp.max(top, axis=-1, keepdims=True))
        alpha = jnp.exp(m_old - m_new)
        ps = [jnp.exp(s - m_new) for s in scores]
        tot, pv = ps[0], _dot(ps[0], values[0])
        for p, c in zip(ps[1:], values[1:]):
            tot = tot + p
            pv = pv + _dot(p, c)
        l_ref[...] = alpha * l_ref[...] + jnp.sum(tot, axis=-1, keepdims=True)
        acc_ref[...] = alpha * acc_ref[...] + pv
        m_ref[...] = m_new

    values = [pages[3 * i][...].astype(BF16) for i in range(pp)]
    lat = [_dot_nt(qlat, cb) for cb in values]
    pe = [_dot(qpe, pages[3 * i + 1][...]) for i in range(pp)]
    scores = [((x + y).reshape(MLA_HEADS, tt, PAGE_SIZE) * pages[3 * i + 2][...][:, None, :]).reshape(rows, PAGE_SIZE)
              for i, (x, y) in enumerate(zip(lat, pe))]
    per = max(1, pp // SOFTMAX_STEPS)
    for i in range(0, pp, per):
        update(scores[i:i + per], values[i:i + per])

    @pl.when(j == pl.num_programs(1) - 1)
    def _():
        pad = lambda x: jnp.concatenate([x, jnp.zeros((PAGE_SIZE - tt, x.shape[1]), F32)], axis=0)
        key = lax.broadcasted_iota(jnp.int32, (rows, PAGE_SIZE), 1)
        tok = lax.broadcasted_iota(jnp.int32, (rows, PAGE_SIZE), 0) % tt
        expand = (lax.broadcasted_iota(jnp.int32, (rows, MLA_HEADS), 0) // tt
                  == lax.broadcasted_iota(jnp.int32, (rows, MLA_HEADS), 1)).astype(BF16)
        cn = pad(kvn_ref[0]).astype(BF16)
        k_hi, k_mid, k_lo = _split3(pad(kin_ref[0]))
        s = _dot_nt(qlat, cn) + _dot_nt(qpe, pad(kpn_ref[0]))
        s = s * (_dot_nt(expand, k_hi) + _dot_nt(expand, k_mid) + _dot_nt(expand, k_lo))
        update([jnp.where(key <= tok, s, NEG_BIG)], [cn])
        o_lat = (acc_ref[...] / l_ref[...]).astype(BF16)
        for pair in range(MLA_HEADS // 2):
            o = jnp.zeros((tt, LANES), F32)
            for h in (2 * pair, 2 * pair + 1):
                o = o + _dot(o_lat[h * tt:(h + 1) * tt, :], wuv_ref[:, h * HEAD_PAD:(h + 1) * HEAD_PAD])
            sl = slice(pair * LANES, (pair + 1) * LANES)
            o_ref[0, :, sl] = (o * _silu(za_ref[0, :, sl])).astype(BF16)


def _attn_sample(page_table, q, kv_new, kpe_new, kinv_new, za, cache_kv, cache_kpe, cache_kinv, w, *, pp):
    nb, tt, _ = q.shape
    n_pages = page_table.shape[1]
    rows = MLA_HEADS * tt
    per_b = lambda width: pl.BlockSpec((1, tt, width), lambda b, j, pt: (b, 0, 0))
    const = lambda shape: pl.BlockSpec(shape, lambda b, j, pt: (0,) * len(shape))

    kpe_t = jnp.swapaxes(cache_kpe, 1, 2)
    kinv_t = jnp.swapaxes(cache_kinv, 1, 2)
    hbm = pl.BlockSpec(memory_space=pl.ANY)
    grid_spec = pltpu.PrefetchScalarGridSpec(
        num_scalar_prefetch=1, grid=(nb, n_pages // pp),
        in_specs=[per_b(QKV_WIDTH), per_b(KV_LORA), per_b(QK_ROPE), per_b(MLA_HEADS), per_b(MLA_WIDTH),
                  const(w["w_ukt"].shape), const(w["g_k_nope"].shape), const(w["w_uv"].shape), hbm, hbm, hbm],
        out_specs=per_b(MLA_WIDTH),
        scratch_shapes=[pltpu.VMEM((2, pp, PAGE_SIZE, KV_LORA), F32), pltpu.VMEM((2, pp, QK_ROPE, PAGE_SIZE), F32),
                        pltpu.VMEM((2, pp, MLA_HEADS, PAGE_SIZE), F32), pltpu.SemaphoreType.DMA((2,)),
                        pltpu.VMEM((rows, KV_LORA), F32), pltpu.VMEM((rows, QK_ROPE), F32),
                        pltpu.VMEM((rows, 1), F32), pltpu.VMEM((rows, 1), F32), pltpu.VMEM((rows, KV_LORA), F32)])
    return pl.pallas_call(
        functools.partial(_attn_sample_kernel, pp=pp, tt=tt),
        grid_spec=grid_spec,
        out_shape=jax.ShapeDtypeStruct((nb, tt, MLA_WIDTH), BF16),
        name="attn_sample", compiler_params=_params("arbitrary", "arbitrary"),
    )(page_table, q, kv_new, kpe_new, kinv_new, za, w["w_ukt"], w["g_k_nope"], w["w_uv"], cache_kv, kpe_t, kinv_t)


def _wkv_kernel(*refs, tk, n_chunks, has_state):
    if has_state:
        (r_ref, k_ref, v_ref, ld_ref, a_ref, zb_ref, kk_ref, ka_ref, rk_ref, gg_ref, gb_ref, s0_ref,
         o_ref, so_ref, s_ref) = refs
    else:
        (r_ref, k_ref, v_ref, ld_ref, a_ref, zb_ref, kk_ref, ka_ref, rk_ref, gg_ref, gb_ref,
         o_ref, so_ref, s_ref) = refs
    C, L, G = WKV_CHUNK, GROUP_LANES, WKV_GROUP
    GC = G * C
    n_groups = RW_WIDTH // L
    ti = pl.program_id(1)
    ri = lax.broadcasted_iota(jnp.int32, (GC, L), 0)
    ci = lax.broadcasted_iota(jnp.int32, (GC, L), 1)
    blk = lambda n: (ri // n) == (ci // n)
    same_head = blk(RW_HEAD).astype(F32)
    t2 = lax.broadcasted_iota(jnp.int32, (C, GC), 0)
    s2 = lax.broadcasted_iota(jnp.int32, (C, GC), 1) % C
    strict2 = (s2 < t2).astype(F32)
    incl2 = (s2 <= t2).astype(F32)
    tri2 = jnp.concatenate([strict2, incl2], axis=0)
    blk2 = lambda n: (t2 // n) == (s2 // n)
    neg_m16 = -blk2(16).astype(F32)
    m32_off = (blk2(32) & ~blk2(16)).astype(F32)
    m64_off = (~blk2(32)).astype(F32)
    eye2 = (t2 == s2).astype(F32)
    tri = (lax.broadcasted_iota(jnp.int32, (C, C), 1) <= lax.broadcasted_iota(jnp.int32, (C, C), 0)).astype(BF16)
    ones_bd = same_head.astype(BF16)
    replicate = (lax.broadcasted_iota(jnp.int32, (RW_HEAD, L), 1) % RW_HEAD
                 == lax.broadcasted_iota(jnp.int32, (RW_HEAD, L), 0)).astype(BF16)

    def seg_sum(xs, split):
        n = xs[0].shape[0]
        if not split:
            y = _dot(jnp.concatenate(xs, axis=0), ones_bd)
            return [y[i * n:(i + 1) * n] for i in range(len(xs))]
        parts = [piece for x in xs for piece in _split2(x)]
        y = jnp.dot(jnp.concatenate(parts, axis=0), ones_bd, preferred_element_type=F32)
        return [y[2 * i * n:(2 * i + 1) * n] + y[(2 * i + 1) * n:(2 * i + 2) * n] for i in range(len(xs))]

    half_lanes = lax.broadcasted_iota(jnp.int32, (C, LANES), 1) // RW_HEAD
    half_mask = [(half_lanes == i).astype(BF16) for i in range(LANES // RW_HEAD)]
    zero_slab = jnp.zeros((C, LANES), BF16)

    def expand(x):
        xb = x.astype(BF16)
        blocks = []
        for h in range(G):
            slab, half = divmod(h * RW_HEAD, LANES)
            keep = xb[:, slab * LANES:(slab + 1) * LANES] * half_mask[half // RW_HEAD]
            blocks.append(jnp.concatenate([keep if s == slab else zero_slab for s in range(L // LANES)], axis=1))
        return jnp.concatenate(blocks, axis=0)

    stack = expand

    @pl.when(ti == 0)
    def _():
        if has_state:
            for n in range(s_ref.shape[0]):
                hi, mid, lo = _split3(s0_ref[n // n_groups, (n % n_groups) * L:(n % n_groups + 1) * L, :])
                s_ref[n] = (_dot(hi, replicate) + _dot(mid, replicate) + _dot(lo, replicate)) * same_head
        else:
            s_ref[...] = jnp.zeros(s_ref.shape, F32)

    def each(f, *cols):
        return [f(*args) for args in zip(*cols)]

    groups = list(range(n_groups))
    lanes = [slice(g * L, (g + 1) * L) for g in groups]
    cpi = 4 if n_chunks % 4 == 0 else 1

    def step(i, carry):
        if tk == C:
            rows = [(0, pl.ds(pl.multiple_of((i * cpi + j) * C, C), C)) for j in range(cpi)]
            load = lambda ref: [ref[0, rw, ln] for _, rw in rows for ln in lanes]
        else:
            rows = [(n, slice(0, tk)) for n in range(r_ref.shape[0])]
            load = lambda ref: [jnp.concatenate([ref[n, :, ln], jnp.zeros((C - tk, L), F32)], axis=0)
                                for n, _ in rows for ln in lanes]
        r, k, v, ld, a, zb = (load(ref) for ref in (r_ref, k_ref, v_ref, ld_ref, a_ref, zb_ref))
        k_k, k_a, r_k, gn_g, gn_b = ([ref[:, ln] for _ in rows for ln in lanes]
                                     for ref in (kk_ref, ka_ref, rk_ref, gg_ref, gb_ref))
        ld_split = each(_split2, ld)
        cum = each(lambda hl: _dot(tri, hl[0]) + _dot(tri, hl[1]), ld_split)
        cum_end = each(lambda x: x[C - 1:C, :], cum)
        kk = each(lambda x, y: x * y, k, k_k)
        norm = seg_sum(each(lambda x: x * x, kk), split=True)
        kk = each(lambda x, n: x * lax.rsqrt(jnp.maximum(n, 1e-24)), kk, norm)
        k = each(lambda x, a_, ka: x * (1.0 + (a_ - 1.0) * ka), k, a, k_a)
        b = each(lambda x, y: x * y, kk, a)
        grow = each(lambda x: jnp.exp(-x), cum)
        tail = each(lambda e, x: jnp.exp(e - x), cum_end, cum)
        ar = each(lambda kk_, x, l_, r_: jnp.concatenate([kk_ * jnp.exp(x - l_), r_ * jnp.exp(x)], axis=0).astype(BF16),
                  kk, cum, ld, r)
        bk = each(lambda b_, k_, g_: jnp.concatenate([stack(b_ * g_), stack(k_ * g_)], axis=0),
                  b, k, grow)
        gm = each(_dot_nt, ar, bk)
        vs = each(stack, v)
        l2b = each(lambda m: m[:C, :GC] * strict2, gm)
        kv_part = each(lambda m, vs_: _dot(m[:, GC:] * tri2, vs_), gm, vs)
        m2b = each(lambda m: (m[C:, :GC] * incl2).astype(BF16), gm)
        kb = each(lambda k_, b_, t_: jnp.concatenate([k_ * t_, b_ * t_], axis=0).astype(BF16), k, b, tail)
        decay_end = each(jnp.exp, cum_end)
        rkr = each(lambda r_, k_, rk: r_ * k_ * rk, r, k, r_k)
        p = each(lambda l_: l_ * neg_m16, l2b)
        x = each(lambda p_: eye2 + p_, p)
        p = each(lambda p_: _dot(p_, expand(p_)), p)
        for _ in range(2):
            t = each(lambda x_, p_: _dot(jnp.concatenate([x_, p_], axis=0), expand(p_)), x, p)
            x = each(lambda x_, t_: x_ + t_[:C], x, t)
            p = each(lambda t_: t_[C:], t)
        x = each(lambda x_, p_: x_ + _dot(x_, expand(p_)), x, p)
        for off in (m32_off, m64_off):
            y = each(lambda l_, x_: _dot(l_ * off, expand(x_)), l2b, x)
            x = each(lambda x_, y_: x_ - _dot(x_, expand(y_)), x, y)
        x2 = each(lambda x_: x_.astype(BF16), x)

        def state_part(part, slots):
            s_prev = [s_ref[n] for n in slots]
            ars = each(_dot_nt, ar[part], s_prev)
            rhs = each(lambda s_, kv_: s_[:C] + kv_[:C], ars, kv_part[part])
            u = each(lambda x_, rhs_: _dot(x_, stack(rhs_)), x2[part], rhs)
            out = each(lambda s_, kv_, m_, u_: s_[C:] + kv_[C:] - _dot(m_, stack(u_)), ars, kv_part[part], m2b[part], u)
            ds = each(lambda v_, u_, kb_: _dot(jnp.concatenate([v_, -u_], axis=0).T, kb_), v[part], u, kb[part])
            for n, s_, d_, ds_ in zip(slots, s_prev, decay_end[part], ds):
                s_ref[n] = s_ * d_ + ds_ * same_head
            return out

        if has_state:
            o = state_part(slice(None), list(range(len(ar))))
        else:
            o = []
            for j in range(len(rows)):
                o += state_part(slice(j * n_groups, (j + 1) * n_groups), groups)

        sums = seg_sum(o + rkr, split=False)
        mean, bonus = sums[:len(o)], sums[len(o):]
        dev = each(lambda x_, m: x_ - m * (1.0 / RW_HEAD), o, mean)
        var = seg_sum(each(lambda x_: x_ * x_, dev), split=False)
        for n, (seq, rw, ln) in enumerate((seq, rw, ln) for seq, rw in rows for ln in lanes):
            out = dev[n] * lax.rsqrt(var[n] * (1.0 / RW_HEAD) + GN_EPS) * gn_g[n] + gn_b[n] + bonus[n] * v[n]
            out = (out * _silu(zb[n])).astype(BF16)
            o_ref[seq, rw, ln] = out[:tk] if tk != C else out
        return carry

    if tk != C:
        step(0, 0)
    else:
        lax.fori_loop(0, n_chunks // cpi, step, 0)

    @pl.when(ti == pl.num_programs(1) - 1)
    def _():
        rep_t = (lax.broadcasted_iota(jnp.int32, (L, RW_HEAD), 0) % RW_HEAD
                 == lax.broadcasted_iota(jnp.int32, (L, RW_HEAD), 1)).astype(BF16)
        for n in range(s_ref.shape[0]):
            hi, mid, lo = _split3(s_ref[n])
            so_ref[n // n_groups, (n % n_groups) * L:(n % n_groups + 1) * L, :] = (
                _dot(hi, rep_t) + _dot(mid, rep_t) + _dot(lo, rep_t))


def _wkv(r, k, v, ld, a, zb, w, state):
    nb, t_all, _ = r.shape
    has_state = state is not None
    if has_state:
        tk, tb, bb = t_all, t_all, _tile(nb, WKV_SEQS)
    else:
        tk, tb, bb = WKV_CHUNK, _tile(t_all, 512), 1
    n_groups = RW_WIDTH // GROUP_LANES
    seq = pl.BlockSpec((bb, tb, RW_WIDTH), lambda b, t: (b, t, 0))
    vec = pl.BlockSpec((1, RW_WIDTH), lambda b, t: (0, 0))
    st = pl.BlockSpec((bb, RW_WIDTH, RW_HEAD), lambda b, t: (b, 0, 0))
    in_specs = [seq] * 6 + [vec] * 5 + ([st] if has_state else [])
    args = [r, k, v, ld, a, zb, w["k_k"], w["k_a"], w["r_k"], w["gn_g"], w["gn_b"]] + ([state] if has_state else [])
    return pl.pallas_call(
        functools.partial(_wkv_kernel, tk=tk, n_chunks=tb // tk, has_state=has_state),
        grid=(nb // bb, t_all // tb), in_specs=in_specs, out_specs=[seq, st],
        out_shape=[jax.ShapeDtypeStruct((nb, t_all, RW_WIDTH), BF16),
                   jax.ShapeDtypeStruct((nb, RW_WIDTH, RW_HEAD), F32)],
        scratch_shapes=[pltpu.VMEM((bb * n_groups, GROUP_LANES, GROUP_LANES), F32)],
        name="wkv", compiler_params=_params("parallel", "arbitrary"),
    )(*args)


def _merge_kernel(x_ref, mod_ref, oa_ref, ob_ref, ga_ref, gb_ref, wpa_ref, wpb_ref, wout_ref, y_ref, *, bb, tt):
    tm = bb * tt
    flat = lambda ref: ref[...].reshape(tm, ref.shape[2])
    m = (_sigmoid(flat(ga_ref)) * _dot(flat(oa_ref), wpa_ref[...])
         + _sigmoid(flat(gb_ref)) * _dot(flat(ob_ref), wpb_ref[...]))
    gate = mod_ref[...][:, :, 2 * D_MODEL:]
    y_ref[...] = x_ref[...] + gate * _dot(m, wout_ref[...]).reshape(bb, tt, D_MODEL)


def _merge(x, mod, oa, ob, ga, gb, w, *, bb, tt):
    nb, t_all, _ = x.shape
    tok = pl.BlockSpec((bb, tt, D_MODEL), lambda i, j: (i, j, 0))
    return pl.pallas_call(
        functools.partial(_merge_kernel, bb=bb, tt=tt),
        grid=(nb // bb, t_all // tt),
        in_specs=[tok, pl.BlockSpec((bb, 1, 3 * D_MODEL), lambda i, j: (i, 0, 0)), tok, tok, tok, tok,
                  _const_spec(w["w_pa"].shape), _const_spec(w["w_pb"].shape), _const_spec(w["w_out"].shape)],
        out_specs=tok, out_shape=jax.ShapeDtypeStruct(x.shape, F32),
        name="merge", compiler_params=_params("parallel", "parallel"),
    )(x, mod, oa, ob, ga, gb, w["w_pa"], w["w_pb"], w["w_out"])


def _head_slabs(rope_part, nope_part):
    lead = nope_part.shape[:-1]
    parts = []
    if rope_part is None:
        parts.append(jnp.zeros(lead + (NOPE_OFF,), nope_part.dtype))
    else:
        parts += [rope_part, jnp.zeros(lead + (NOPE_OFF - QK_ROPE,), nope_part.dtype)]
    parts.append(nope_part)
    slab = jnp.concatenate(parts, axis=-1)
    return slab.reshape(lead[:-1] + (MLA_HEADS * HEAD_PAD,))


def _pad_cols(x, width):
    return jnp.pad(x, ((0, 0), (0, width - x.shape[1])))


def _prepare_weights(w_ada, b_ada, g_norm, w_in, g_qa, w_uq, g_kva, w_uk, w_uv, g_q, g_k, w_pa, mu_shift, w0,
                     w_decay2, a0, w_a2, k_k, k_a, r_k, gn_g, gn_b, w_pb, w_out):
    bf = lambda x: x.astype(BF16)
    row = lambda x: x.reshape(1, -1)
    splits = (Q_LORA, KV_LORA, QK_ROPE, MLA_WIDTH, RW_WIDTH, DECAY_LORA, RW_WIDTH, RW_WIDTH, A_LORA, RW_WIDTH,
              D_MODEL, D_MODEL)
    offs = [0]
    for s in splits:
        offs.append(offs[-1] + s)
    cols = [w_in[:, offs[i]:offs[i + 1]] for i in range(len(splits))]
    w_qa, w_kva, w_kpe, w_za, w_r, w_wl, w_k, w_v, w_al, w_zb, w_ga, w_gb = cols
    mu = [mu_shift[offs[i] - offs[4]:offs[i + 1] - offs[4]] for i in range(4, 9)]
    pad_row = lambda x: _pad_cols(row(x), LANES).reshape(1, 1, LANES)
    row3 = lambda x: x.reshape(1, 1, -1)
    zeros_h = jnp.zeros((KV_LORA, MLA_HEADS, V_DIM), F32)
    parity = (jnp.arange(MLA_HEADS) % 2 == 0)[None, :, None]
    w_uv_slab = jnp.concatenate([jnp.where(parity, w_uv, zeros_h), jnp.where(parity, zeros_h, w_uv)], axis=-1)
    w_ukt = jnp.pad(jnp.transpose(w_uk, (1, 2, 0)), ((0, 0), (NOPE_OFF, 0), (0, 0)))
    swap_cols = lambda x: jnp.concatenate([-x[..., HALF_ROPE:], x[..., :HALF_ROPE]], axis=-1)
    swap_gain = lambda g: jnp.pad(jnp.concatenate([g[HALF_ROPE:], g[:HALF_ROPE]]), (0, HEAD_PAD - QK_ROPE))
    w_uq_swap = _head_slabs(swap_cols(w_uq[..., QK_NOPE:]), jnp.zeros_like(w_uq[..., :QK_NOPE]))
    w_kpe_both = jnp.concatenate([_pad_cols(w_kpe, HEAD_PAD), _pad_cols(swap_cols(w_kpe), HEAD_PAD)], axis=1)
    g_k_rope = jnp.pad(g_k[QK_NOPE:], (0, HEAD_PAD - QK_ROPE))
    g_k_nope = jnp.pad(g_k[:QK_NOPE], (NOPE_OFF, 0))
    g_q_slab = jnp.concatenate([g_q[QK_NOPE:], jnp.zeros((NOPE_OFF - QK_ROPE,), F32), g_q[:QK_NOPE]])
    return {
        "w_ada": bf(w_ada), "b_ada": row(b_ada), "g_norm": row3(g_norm),
        "w_qa": bf(w_qa), "w_kva": bf(w_kva), "w_kpe": bf(w_kpe_both), "w_za": bf(w_za),
        "w_ga": bf(w_ga), "w_gb": bf(w_gb), "g_qa": row(g_qa), "g_kva": row(g_kva),
        "w_uq": bf(_head_slabs(w_uq[..., QK_NOPE:], w_uq[..., :QK_NOPE])),
        "w_uk": bf(_head_slabs(None, w_uk)), "w_uv": bf(w_uv_slab.reshape(KV_LORA, QKV_WIDTH)),
        "w_uq_swap": bf(w_uq_swap), "w_ukt": bf(w_ukt), "g_q": row(g_q_slab),
        "g_q_swap": row(swap_gain(g_q[QK_NOPE:])), "w_uv_t": bf(w_uv.reshape(KV_LORA, MLA_WIDTH).T),
        "g_k_rope": row(g_k_rope), "g_k_rope_swap": row(swap_gain(g_k[QK_NOPE:])), "g_k_nope": row(g_k_nope),
        "w_r": bf(w_r), "w_wl": bf(_pad_cols(w_wl, LANES)), "w_k": bf(w_k), "w_v": bf(w_v),
        "w_al": bf(_pad_cols(w_al, LANES)), "w_zb": bf(w_zb),
        "mu_r": row3(mu[0]), "mu_w": pad_row(mu[1]), "mu_k": row3(mu[2]), "mu_v": row3(mu[3]), "mu_a": pad_row(mu[4]),
        "w_decay2": bf(jnp.pad(w_decay2, ((0, LANES - DECAY_LORA), (0, 0)))), "w0": row(w0),
        "w_a2": bf(jnp.pad(w_a2, ((0, LANES - A_LORA), (0, 0)))), "a0": row(a0),
        "k_k": row(k_k), "k_a": row(k_a), "r_k": row(r_k), "gn_g": row(gn_g), "gn_b": row(gn_b),
        "w_pa": bf(w_pa), "w_pb": bf(w_pb), "w_out": bf(w_out),
    }


def _rope_tables(pos):
    inv = ROPE_THETA ** (-jnp.arange(HALF_ROPE, dtype=F32) / HALF_ROPE)
    ang = pos.astype(F32)[:, None] * inv[None, :]
    cos, sin = jnp.cos(ang), jnp.sin(ang)
    n = pos.shape[0]
    rest = jnp.zeros((n, HEAD_PAD - QK_ROPE), F32)
    return jnp.concatenate([cos, cos, rest + 1.0], axis=1), jnp.concatenate([sin, sin, rest], axis=1)


def _split_shift(state_shift):
    o1 = RW_WIDTH
    o2 = o1 + DECAY_LORA
    o3 = o2 + RW_WIDTH
    o4 = o3 + RW_WIDTH
    n = state_shift.shape[0]
    pad = lambda x: _pad_cols(x, LANES)
    parts = (state_shift[:, :o1], pad(state_shift[:, o1:o2]), state_shift[:, o2:o3], state_shift[:, o3:o4],
             pad(state_shift[:, o4:]))
    return [p.reshape(n, 1, -1) for p in parts]


def _join_shift(last):
    lr, lw, lk, lv, la = (x[:, 0, :] for x in last)
    return jnp.concatenate([lr, lw[:, :DECAY_LORA], lk, lv, la[:, :A_LORA]], axis=1)


def _tile(n, target):
    t = min(n, target)
    while n % t:
        t //= 2
    return t


def kernel(x_prompt, x_sample, c_prompt, c_sample, cache_kv, cache_kpe, cache_kinv, state_wkv, state_shift, page_table, w_ada, b_ada, g_norm, w_in, g_qa, w_uq, g_kva, w_uk, w_uv, g_q, g_k, w_pa, mu_shift, w0, w_decay2, a0, w_a2, k_k, k_a, r_k, gn_g, gn_b, w_pb, w_out):
    B, S, _ = x_prompt.shape
    Bd, T, _ = x_sample.shape
    n_pages = page_table.shape[1]
    past = n_pages * PAGE_SIZE
    w = _prepare_weights(w_ada, b_ada, g_norm, w_in, g_qa, w_uq, g_kva, w_uk, w_uv, g_q, g_k, w_pa, mu_shift, w0,
                         w_decay2, a0, w_a2, k_k, k_a, r_k, gn_g, gn_b, w_pb, w_out)

    mod = _mod(jnp.concatenate([c_prompt, c_sample], axis=0), w["w_ada"], w["b_ada"])
    mod_p = mod[:B].reshape(B, 1, 3 * D_MODEL)
    mod_s = mod[B:].reshape(Bd, 1, 3 * D_MODEL)

    tt = _tile(S, 256)
    q, kv_p, kpe_p, kinv_p, za, ga, gb, keys, vals = _front_mla(
        x_prompt, mod_p, _rope_tables(jnp.arange(S)), w, SCALE * LOG2_E, bb=1, tt=tt, q_dtype=BF16, emit_kv=True)
    zero_shift = _split_shift(jnp.zeros((B, 3 * RW_WIDTH + DECAY_LORA + A_LORA), F32))
    r, k, v, ld, a, zb, *last_p = _front_rwkv(x_prompt, mod_p, zero_shift, w, bb=1, tt=_tile(S, 512))
    o_a = _attn_prompt(q, keys, vals, za, tq=_tile(S, 256), tk=_tile(S, 256))
    o_b, wkv_p = _wkv(r, k, v, ld, a, zb, w, None)
    y_prompt = _merge(x_prompt, mod_p, o_a, o_b, ga, gb, w, bb=1, tt=_tile(S, 512))

    bs = _tile(Bd, 32)
    q_s, kv_s, kpe_s, kinv_s, za_s, ga_s, gb_s = _front_mla(
        x_sample, mod_s, _rope_tables(past + jnp.arange(T)), w, SCALE, bb=bs, tt=T, q_dtype=F32, emit_kv=False)
    r_s, k_s, v_s, ld_s, a_s, zb_s, *last_s = _front_rwkv(x_sample, mod_s, _split_shift(state_shift), w, bb=bs, tt=T)
    o_a_s = _attn_sample(page_table, q_s, kv_s, kpe_s, kinv_s, za_s, cache_kv, cache_kpe, cache_kinv, w,
                         pp=_tile(n_pages, 16))
    o_b_s, wkv_s = _wkv(r_s, k_s, v_s, ld_s, a_s, zb_s, w, state_wkv.reshape(Bd, RW_WIDTH, RW_HEAD))
    y_sample = _merge(x_sample, mod_s, o_a_s, o_b_s, ga_s, gb_s, w, bb=bs, tt=T)

    return (y_prompt, y_sample, kv_p, kpe_p, kinv_p, wkv_p.reshape(B, RW_HEADS, RW_HEAD, RW_HEAD),
            _join_shift(last_p), kv_s, kpe_s, kinv_s, wkv_s.reshape(Bd, RW_HEADS, RW_HEAD, RW_HEAD),
            _join_shift(last_s))
```

```python
import functools
import math

import jax
import jax.numpy as jnp
from jax import lax
from jax.experimental import pallas as pl
from jax.experimental.pallas import tpu as pltpu

F32 = jnp.float32
BF16 = jnp.bfloat16

D_MODEL = 1024
PAGE_SIZE = 128
MLA_HEADS = 16
QK_NOPE = 64
QK_ROPE = 32
QK_DIM = QK_NOPE + QK_ROPE
V_DIM = 64
Q_LORA = 384
KV_LORA = 256
ROPE_THETA = 10000.0
MLA_WIDTH = MLA_HEADS * V_DIM
SCALE = QK_DIM ** -0.5
RW_HEADS = 16
RW_HEAD = 64
RW_WIDTH = RW_HEADS * RW_HEAD
DECAY_LORA = 64
A_LORA = 64
RMS_EPS = 1e-6
GN_EPS = 64e-5

LANES = 128
HEAD_PAD = LANES
QKV_WIDTH = MLA_HEADS * HEAD_PAD
NOPE_OFF = HEAD_PAD - QK_NOPE
HALF_ROPE = QK_ROPE // 2
WKV_CHUNK = 64
WKV_GROUP = 4
GROUP_LANES = WKV_GROUP * RW_HEAD
VMEM_LIMIT = 56 * 1024 * 1024
NEG_BIG = -1e30
DECAY_SCALE = math.exp(-0.5)
LOG2_E = math.log2(math.e)
HEADS_PER_STAGE = 16
ACC_ROWS = V_DIM + 16
WKV_SEQS = 4
SOFTMAX_STEPS = 2
GATE_SLICE = 256


def _dot(a, b):
    return jnp.dot(a.astype(BF16), b.astype(BF16), preferred_element_type=F32)


def _dot_nt(a, b):
    return lax.dot_general(a.astype(BF16), b.astype(BF16), (((1,), (1,)), ((), ())), preferred_element_type=F32)


def _split2(x):
    hi = x.astype(BF16)
    lo = (x - hi.astype(F32)).astype(BF16)
    return hi, lo


def _split3(x):
    hi = x.astype(BF16)
    r1 = x - hi.astype(F32)
    mid = r1.astype(BF16)
    lo = (r1 - mid.astype(F32)).astype(BF16)
    return hi, mid, lo


def _sigmoid(x):
    return 1.0 / (1.0 + jnp.exp(-x))


def _silu(x):
    return x * _sigmoid(x)


def _rms(x, eps=RMS_EPS):
    return x * lax.rsqrt(jnp.mean(x * x, axis=-1, keepdims=True) + eps)


def _params(*sem):
    return pltpu.CompilerParams(dimension_semantics=sem, vmem_limit_bytes=VMEM_LIMIT)


def _const_spec(shape):
    nd = len(shape)
    return pl.BlockSpec(shape, lambda *_: (0,) * nd)


def _mod_kernel(c_ref, w_ref, b_ref, o_ref):
    o_ref[...] = _dot(_silu(c_ref[...]), w_ref[...]) + b_ref[...]


def _mod(c, w_ada, b_ada):
    n = c.shape[0]
    return pl.pallas_call(
        _mod_kernel,
        out_shape=jax.ShapeDtypeStruct((n, 3 * D_MODEL), F32),
        name="mod",
        compiler_params=pltpu.CompilerParams(vmem_limit_bytes=VMEM_LIMIT),
    )(c, w_ada, b_ada)


def _modulated(x_ref, mod_ref, gn_ref, bb, tt):
    x = x_ref[...]
    mod = mod_ref[...]
    shift = mod[:, :, :D_MODEL]
    scale = mod[:, :, D_MODEL:2 * D_MODEL]
    h = _rms(x) * gn_ref[...] * (1.0 + scale) + shift
    return h.reshape(bb * tt, D_MODEL).astype(BF16)


def _front_mla_kernel(x_ref, mod_ref, gn_ref, cos_ref, sin_ref,
                      wqa_ref, wkva_ref, wkpe_ref, wza_ref, wga_ref, wgb_ref,
                      gqa_ref, wuq_ref, wuqs_ref, gq_ref, gqs_ref, gkva_ref, wuk_ref, gkr_ref, gkrs_ref, gkn_ref,
                      wuvt_ref, *out_refs, bb, tt, emit_kv):
    if emit_kv:
        q_ref, kv_ref, kpe_ref, kinv_ref, za_ref, ga_ref, gb_ref, k_ref, vt_ref = out_refs
    else:
        q_ref, kv_ref, kpe_ref, kinv_ref, za_ref, ga_ref, gb_ref = out_refs
    tm = bb * tt
    hb = _modulated(x_ref, mod_ref, gn_ref, bb, tt)

    def table(ref):
        return jnp.broadcast_to(ref[...][None], (bb, tt, HEAD_PAD)).reshape(tm, HEAD_PAD)

    cos, sin = table(cos_ref), table(sin_ref)

    gate_slices = [(o_ref, w_ref, c) for o_ref, w_ref in ((za_ref, wza_ref), (ga_ref, wga_ref), (gb_ref, wgb_ref))
                   for c in range(0, D_MODEL, GATE_SLICE)]
    n_stages = 2 * MLA_HEADS

    def gates_due(stage):
        lo = stage * len(gate_slices) // n_stages
        hi = (stage + 1) * len(gate_slices) // n_stages
        for o_ref, w_ref, c in gate_slices[lo:hi]:
            o_ref[:, :, c:c + GATE_SLICE] = _dot(hb, w_ref[:, c:c + GATE_SLICE]).reshape(bb, tt, GATE_SLICE)

    qn = (_rms(_dot(hb, wqa_ref[...])) * gqa_ref[...]).astype(BF16)
    qf = _dot(qn, wuq_ref[...])
    qs = _dot(qn, wuqs_ref[...])
    c_lat = _rms(_dot(hb, wkva_ref[...])) * gkva_ref[...]
    kv_ref[...] = c_lat.reshape(bb, tt, KV_LORA)
    cb = c_lat.astype(BF16)
    kpe2 = _dot(hb, wkpe_ref[...])
    kpe_raw = kpe2[:, :HEAD_PAD]
    kp2 = jnp.sum(kpe_raw * kpe_raw, axis=-1, keepdims=True)
    kpe_rot = kpe_raw * (gkr_ref[...] * cos) + kpe2[:, HEAD_PAD:] * (gkrs_ref[...] * sin)
    kpe_ref[...] = kpe_rot[:, :QK_ROPE].reshape(bb, tt, QK_ROPE)
    knf = _dot(cb, wuk_ref[...])
    if emit_kv:
        vt_ref[0, 0] = _dot_nt(wuvt_ref[...], cb).astype(BF16)

    q_cos = gq_ref[...] * cos
    q_sin = gqs_ref[...] * sin
    for h in range(MLA_HEADS):
        sl = slice(h * HEAD_PAD, (h + 1) * HEAD_PAD)
        qh = qf[:, sl]
        ms = jnp.sum(qh * qh, axis=-1, keepdims=True) * (1.0 / QK_DIM)
        qh = lax.rsqrt(ms + RMS_EPS) * (qh * q_cos + qs[:, sl] * q_sin)
        q_ref[:, :, sl] = qh.reshape(bb, tt, HEAD_PAD).astype(q_ref.dtype)
        gates_due(h)

    gkn = gkn_ref[...]
    head_lane = lax.broadcasted_iota(jnp.int32, (tm, MLA_HEADS), 1)
    kinv_all = jnp.zeros((tm, MLA_HEADS), F32)
    for h in range(MLA_HEADS):
        sl = slice(h * HEAD_PAD, (h + 1) * HEAD_PAD)
        kn = knf[:, sl]
        ms = (jnp.sum(kn * kn, axis=-1, keepdims=True) + kp2) * (1.0 / QK_DIM)
        kinv = lax.rsqrt(ms + RMS_EPS)
        kinv_all = jnp.where(head_lane == h, kinv, kinv_all)
        if emit_kv:
            k_ref[:, :, sl] = ((kn * gkn + kpe_rot) * kinv).reshape(bb, tt, HEAD_PAD).astype(BF16)
        gates_due(MLA_HEADS + h)
    kinv_ref[...] = kinv_all.reshape(bb, tt, MLA_HEADS)


def _front_mla(x, mod, tabs, w, q_gain, *, bb, tt, q_dtype, emit_kv):
    nb, t_all, _ = x.shape
    grid = (nb // bb, t_all // tt)
    tok = lambda width: pl.BlockSpec((bb, tt, width), lambda i, j: (i, j, 0))
    tab = pl.BlockSpec((tt, HEAD_PAD), lambda i, j: (j, 0))
    weights = [w["w_qa"], w["w_kva"], w["w_kpe"], w["w_za"], w["w_ga"], w["w_gb"], w["g_qa"], w["w_uq"],
               w["w_uq_swap"], w["g_q"] * q_gain, w["g_q_swap"] * q_gain, w["g_kva"], w["w_uk"], w["g_k_rope"],
               w["g_k_rope_swap"], w["g_k_nope"], w["w_uv_t"]]
    in_specs = ([tok(D_MODEL), pl.BlockSpec((bb, 1, 3 * D_MODEL), lambda i, j: (i, 0, 0)),
                 _const_spec((1, 1, D_MODEL)), tab, tab] + [_const_spec(a.shape) for a in weights])
    sds = lambda width, dt=F32: jax.ShapeDtypeStruct((nb, t_all, width), dt)
    out_shape = [sds(QKV_WIDTH, q_dtype), sds(KV_LORA), sds(QK_ROPE), sds(MLA_HEADS), sds(MLA_WIDTH), sds(D_MODEL),
                 sds(D_MODEL)]
    out_specs = [tok(QKV_WIDTH), tok(KV_LORA), tok(QK_ROPE), tok(MLA_HEADS), tok(MLA_WIDTH), tok(D_MODEL),
                 tok(D_MODEL)]
    if emit_kv:
        assert bb == 1
        out_shape += [sds(QKV_WIDTH, BF16), jax.ShapeDtypeStruct((nb, t_all // tt, MLA_WIDTH, tt), BF16)]
        out_specs += [tok(QKV_WIDTH), pl.BlockSpec((1, 1, MLA_WIDTH, tt), lambda i, j: (i, j, 0, 0))]
    return pl.pallas_call(
        functools.partial(_front_mla_kernel, bb=bb, tt=tt, emit_kv=emit_kv),
        grid=grid, in_specs=in_specs, out_specs=out_specs, out_shape=out_shape,
        name="front_mla", compiler_params=_params("parallel", "arbitrary"),
    )(x, mod, w["g_norm"], *tabs, *weights)


def _front_rwkv_kernel(x_ref, mod_ref, gn_ref, s0r_ref, s0w_ref, s0k_ref, s0v_ref, s0a_ref,
                       mur_ref, muw_ref, muk_ref, muv_ref, mua_ref,
                       wr_ref, ww_ref, wk_ref, wv_ref, wa_ref, wzb_ref,
                       wd2_ref, w0_ref, wa2_ref, a0_ref,
                       r_ref, k_ref, v_ref, ld_ref, a_ref, zb_ref,
                       lr_ref, lw_ref, lk_ref, lv_ref, la_ref,
                       cr_ref, cw_ref, ck_ref, cv_ref, ca_ref, *, bb, tt):
    j = pl.program_id(1)
    hb = _modulated(x_ref, mod_ref, gn_ref, bb, tt)

    def shifted(w_ref, s0_ref, mu_ref, last_ref, carry_ref):
        width = w_ref.shape[1]
        p = _dot(hb, w_ref[...]).reshape(bb, tt, width)
        first = jnp.where(j == 0, s0_ref[...], carry_ref[...])
        t_idx = lax.broadcasted_iota(jnp.int32, (bb, tt, width), 1)
        prev = jnp.where(t_idx == 0, first, pltpu.roll(p, 1, 1))
        last = p[:, tt - 1:tt, :]
        last_ref[...] = last
        carry_ref[...] = last
        return (p + (prev - p) * mu_ref[...]).reshape(bb * tt, width)

    r_ref[...] = shifted(wr_ref, s0r_ref, mur_ref, lr_ref, cr_ref).reshape(bb, tt, RW_WIDTH)
    k_ref[...] = shifted(wk_ref, s0k_ref, muk_ref, lk_ref, ck_ref).reshape(bb, tt, RW_WIDTH)
    v_ref[...] = shifted(wv_ref, s0v_ref, muv_ref, lv_ref, cv_ref).reshape(bb, tt, RW_WIDTH)
    wl = shifted(ww_ref, s0w_ref, muw_ref, lw_ref, cw_ref)
    al = shifted(wa_ref, s0a_ref, mua_ref, la_ref, ca_ref)
    y = w0_ref[...] + _dot(jnp.tanh(wl), wd2_ref[...])
    ld_ref[...] = (-DECAY_SCALE * _sigmoid(y)).reshape(bb, tt, RW_WIDTH)
    a_ref[...] = _sigmoid(a0_ref[...] + _dot(al, wa2_ref[...])).reshape(bb, tt, RW_WIDTH)
    zb_ref[...] = _dot(hb, wzb_ref[...]).reshape(bb, tt, RW_WIDTH)


def _front_rwkv(x, mod, s0, w, *, bb, tt):
    nb, t_all, _ = x.shape
    grid = (nb // bb, t_all // tt)
    tok = lambda width: pl.BlockSpec((bb, tt, width), lambda i, j: (i, j, 0))
    row = lambda width: pl.BlockSpec((bb, 1, width), lambda i, j: (i, 0, 0))
    widths = (RW_WIDTH, LANES, RW_WIDTH, RW_WIDTH, LANES)
    mus = [w["mu_r"], w["mu_w"], w["mu_k"], w["mu_v"], w["mu_a"]]
    weights = [w["w_r"], w["w_wl"], w["w_k"], w["w_v"], w["w_al"], w["w_zb"], w["w_decay2"], w["w0"], w["w_a2"],
               w["a0"]]
    in_specs = ([tok(D_MODEL), row(3 * D_MODEL), _const_spec((1, 1, D_MODEL))] + [row(n) for n in widths]
                + [_const_spec(a.shape) for a in mus] + [_const_spec(a.shape) for a in weights])
    out_shape = ([jax.ShapeDtypeStruct((nb, t_all, RW_WIDTH), F32)] * 6
                 + [jax.ShapeDtypeStruct((nb, 1, n), F32) for n in widths])
    out_specs = [tok(RW_WIDTH)] * 6 + [row(n) for n in widths]
    scratch = [pltpu.VMEM((bb, 1, n), F32) for n in widths]
    return pl.pallas_call(
        functools.partial(_front_rwkv_kernel, bb=bb, tt=tt),
        grid=grid, in_specs=in_specs, out_specs=out_specs, out_shape=out_shape, scratch_shapes=scratch,
        name="front_rwkv", compiler_params=_params("parallel", "arbitrary"),
    )(x, mod, w["g_norm"], *s0, *mus, *weights)


def _attn_prompt_kernel(q_ref, k_ref, vt_ref, za_ref, o_ref, qt_ref, m_ref, acc_ref, *, tq, tk):
    qi = pl.program_id(1)
    n_diag = tq // tk
    n_full = qi * n_diag
    key = lax.broadcasted_iota(jnp.int32, (tk, tq), 0)
    qry = lax.broadcasted_iota(jnp.int32, (tk, tq), 1)
    for h in range(MLA_HEADS):
        qt_ref[h] = q_ref[0, :, h * HEAD_PAD:(h + 1) * HEAD_PAD].astype(F32).T.astype(BF16)
    m_ref[...] = jnp.full(m_ref.shape, NEG_BIG, F32)
    ones_rows = jnp.ones((ACC_ROWS - V_DIM, tk), BF16)
    acc_ref[...] = jnp.zeros(acc_ref.shape, F32)

    def step(j, diag):
        rows = pl.ds(pl.multiple_of(j * tk, tk), tk)
        allowed = None if diag is None else key + diag * tk <= qry
        for h0 in range(0, MLA_HEADS, HEADS_PER_STAGE):
            heads = list(range(h0, h0 + HEADS_PER_STAGE))
            s = [_dot(k_ref[0, rows, h * HEAD_PAD:(h + 1) * HEAD_PAD], qt_ref[h]) for h in heads]
            if allowed is not None:
                s = [jnp.where(allowed, x, NEG_BIG) for x in s]
            m_old = [m_ref[h] for h in heads]
            m_new = [jnp.maximum(mo, jnp.max(x, axis=0, keepdims=True)) for mo, x in zip(m_old, s)]
            p = [jnp.exp2(x - mn) for x, mn in zip(s, m_new)]
            pv = [_dot(jnp.concatenate([vt_ref[0, j, h * V_DIM:(h + 1) * V_DIM, :], ones_rows], axis=0), x)
                  for h, x in zip(heads, p)]
            for h, mo, mn, y in zip(heads, m_old, m_new, pv):
                acc_ref[h] = jnp.exp2(mo - mn) * acc_ref[h] + y
                m_ref[h] = mn

    def full_block(j, carry):
        step(j, None)
        return carry

    lax.fori_loop(0, n_full, full_block, 0)
    for d in range(n_diag):
        step(n_full + d, d)

    for pair in range(MLA_HEADS // 2):
        o_t = jnp.concatenate([acc_ref[h, :V_DIM, :] / acc_ref[h, V_DIM:V_DIM + 1, :]
                               for h in (2 * pair, 2 * pair + 1)], axis=0)
        sl = slice(pair * LANES, (pair + 1) * LANES)
        o_ref[0, :, sl] = (o_t.T * _silu(za_ref[0, :, sl])).astype(BF16)


def _attn_prompt(q, k, vt, za, *, tq, tk):
    nb, s_len, _ = q.shape
    return pl.pallas_call(
        functools.partial(_attn_prompt_kernel, tq=tq, tk=tk),
        grid=(nb, s_len // tq),
        in_specs=[pl.BlockSpec((1, tq, QKV_WIDTH), lambda b, i: (b, i, 0)),
                  pl.BlockSpec((1, s_len, QKV_WIDTH), lambda b, i: (b, 0, 0)),
                  pl.BlockSpec((1, s_len // tk, MLA_WIDTH, tk), lambda b, i: (b, 0, 0, 0)),
                  pl.BlockSpec((1, tq, MLA_WIDTH), lambda b, i: (b, i, 0))],
        out_specs=pl.BlockSpec((1, tq, MLA_WIDTH), lambda b, i: (b, i, 0)),
        out_shape=jax.ShapeDtypeStruct((nb, s_len, MLA_WIDTH), BF16),
        scratch_shapes=[pltpu.VMEM((MLA_HEADS, HEAD_PAD, tq), BF16), pltpu.VMEM((MLA_HEADS, 1, tq), F32),
                        pltpu.VMEM((MLA_HEADS, ACC_ROWS, tq), F32)],
        name="attn_prompt", compiler_params=_params("parallel", "arbitrary"),
    )(q, k, vt, za)


def _attn_sample_kernel(pt_ref, q_ref, kvn_ref, kpn_ref, kin_ref, za_ref, wukt_ref, gkn_ref, wuv_ref,
                        ckv_hbm, ckpe_hbm, ckinv_hbm, o_ref,
                        kv_buf, kpe_buf, kinv_buf, sem, qlat_ref, qpe_ref, m_ref, l_ref, acc_ref, *, pp, tt):
    b, j = pl.program_id(0), pl.program_id(1)
    nb, nj = pl.num_programs(0), pl.num_programs(1)
    rows = MLA_HEADS * tt
    step = b * nj + j
    slot = lax.rem(step, 2)

    def page_copies(bb, jj, into):
        out = []
        for i in range(pp):
            page = pt_ref[bb, jj * pp + i]
            out += [pltpu.make_async_copy(ckv_hbm.at[page], kv_buf.at[into, i], sem.at[into]),
                    pltpu.make_async_copy(ckpe_hbm.at[page], kpe_buf.at[into, i], sem.at[into]),
                    pltpu.make_async_copy(ckinv_hbm.at[page], kinv_buf.at[into, i], sem.at[into])]
        return out

    @pl.when(step == 0)
    def _():
        for copy in page_copies(b, j, slot):
            copy.start()

    @pl.when(step + 1 < nb * nj)
    def _():
        wrap = j + 1 == nj
        for copy in page_copies(jnp.where(wrap, b + 1, b), jnp.where(wrap, 0, j + 1), 1 - slot):
            copy.start()

    for copy in page_copies(b, j, slot):
        copy.wait()
    pages = [buf.at[slot, i] for i in range(pp) for buf in (kv_buf, kpe_buf, kinv_buf)]

    @pl.when(j == 0)
    def _():
        gkn = gkn_ref[...]
        for h in range(MLA_HEADS):
            qh = q_ref[0, :, h * HEAD_PAD:(h + 1) * HEAD_PAD]
            qlat_ref[h * tt:(h + 1) * tt, :] = _dot(qh * gkn, wukt_ref[h])
            qpe_ref[h * tt:(h + 1) * tt, :] = qh[:, :QK_ROPE]
        m_ref[...] = jnp.full((rows, 1), NEG_BIG, F32)
        l_ref[...] = jnp.zeros((rows, 1), F32)
        acc_ref[...] = jnp.zeros((rows, KV_LORA), F32)

    qlat = qlat_ref[...].astype(BF16)
    qpe = qpe_ref[...].astype(BF16)

    def update(scores, values):
        m_old = m_ref[...]
        top = scores[0]
        for s in scores[1:]:
            top = jnp.maximum(top, s)
        m_new = jnp.maximum(m_old, jnp.max(top, axis=-1, keepdims=True))
        alpha = jnp.exp(m_old - m_new)
        ps = [jnp.exp(s - m_new) for s in scores]
        tot, pv = ps[0], _dot(ps[0], values[0])
        for p, c in zip(ps[1:], values[1:]):
            tot = tot + p
            pv = pv + _dot(p, c)
        l_ref[...] = alpha * l_ref[...] + jnp.sum(tot, axis=-1, keepdims=True)
        acc_ref[...] = alpha * acc_ref[...] + pv
        m_ref[...] = m_new

    values = [pages[3 * i][...].astype(BF16) for i in range(pp)]
    lat = [_dot_nt(qlat, cb) for cb in values]
    pe = [_dot(qpe, pages[3 * i + 1][...]) for i in range(pp)]
    scores = [((x + y).reshape(MLA_HEADS, tt, PAGE_SIZE) * pages[3 * i + 2][...][:, None, :]).reshape(rows, PAGE_SIZE)
              for i, (x, y) in enumerate(zip(lat, pe))]
    per = max(1, pp // SOFTMAX_STEPS)
    for i in range(0, pp, per):
        update(scores[i:i + per], values[i:i + per])

    @pl.when(j == pl.num_programs(1) - 1)
    def _():
        pad = lambda x: jnp.concatenate([x, jnp.zeros((PAGE_SIZE - tt, x.shape[1]), F32)], axis=0)
        key = lax.broadcasted_iota(jnp.int32, (rows, PAGE_SIZE), 1)
        tok = lax.broadcasted_iota(jnp.int32, (rows, PAGE_SIZE), 0) % tt
        expand = (lax.broadcasted_iota(jnp.int32, (rows, MLA_HEADS), 0) // tt
                  == lax.broadcasted_iota(jnp.int32, (rows, MLA_HEADS), 1)).astype(BF16)
        cn = pad(kvn_ref[0]).astype(BF16)
        k_hi, k_mid, k_lo = _split3(pad(kin_ref[0]))
        s = _dot_nt(qlat, cn) + _dot_nt(qpe, pad(kpn_ref[0]))
        s = s * (_dot_nt(expand, k_hi) + _dot_nt(expand, k_mid) + _dot_nt(expand, k_lo))
        update([jnp.where(key <= tok, s, NEG_BIG)], [cn])
        o_lat = (acc_ref[...] / l_ref[...]).astype(BF16)
        for pair in range(MLA_HEADS // 2):
            o = jnp.zeros((tt, LANES), F32)
            for h in (2 * pair, 2 * pair + 1):
                o = o + _dot(o_lat[h * tt:(h + 1) * tt, :], wuv_ref[:, h * HEAD_PAD:(h + 1) * HEAD_PAD])
            sl = slice(pair * LANES, (pair + 1) * LANES)
            o_ref[0, :, sl] = (o * _silu(za_ref[0, :, sl])).astype(BF16)


def _attn_sample(page_table, q, kv_new, kpe_new, kinv_new, za, cache_kv, cache_kpe, cache_kinv, w, *, pp):
    nb, tt, _ = q.shape
    n_pages = page_table.shape[1]
    rows = MLA_HEADS * tt
    per_b = lambda width: pl.BlockSpec((1, tt, width), lambda b, j, pt: (b, 0, 0))
    const = lambda shape: pl.BlockSpec(shape, lambda b, j, pt: (0,) * len(shape))

    kpe_t = jnp.swapaxes(cache_kpe, 1, 2)
    kinv_t = jnp.swapaxes(cache_kinv, 1, 2)
    hbm = pl.BlockSpec(memory_space=pl.ANY)
    grid_spec = pltpu.PrefetchScalarGridSpec(
        num_scalar_prefetch=1, grid=(nb, n_pages // pp),
        in_specs=[per_b(QKV_WIDTH), per_b(KV_LORA), per_b(QK_ROPE), per_b(MLA_HEADS), per_b(MLA_WIDTH),
                  const(w["w_ukt"].shape), const(w["g_k_nope"].shape), const(w["w_uv"].shape), hbm, hbm, hbm],
        out_specs=per_b(MLA_WIDTH),
        scratch_shapes=[pltpu.VMEM((2, pp, PAGE_SIZE, KV_LORA), F32), pltpu.VMEM((2, pp, QK_ROPE, PAGE_SIZE), F32),
                        pltpu.VMEM((2, pp, MLA_HEADS, PAGE_SIZE), F32), pltpu.SemaphoreType.DMA((2,)),
                        pltpu.VMEM((rows, KV_LORA), F32), pltpu.VMEM((rows, QK_ROPE), F32),
                        pltpu.VMEM((rows, 1), F32), pltpu.VMEM((rows, 1), F32), pltpu.VMEM((rows, KV_LORA), F32)])
    return pl.pallas_call(
        functools.partial(_attn_sample_kernel, pp=pp, tt=tt),
        grid_spec=grid_spec,
        out_shape=jax.ShapeDtypeStruct((nb, tt, MLA_WIDTH), BF16),
        name="attn_sample", compiler_params=_params("arbitrary", "arbitrary"),
    )(page_table, q, kv_new, kpe_new, kinv_new, za, w["w_ukt"], w["g_k_nope"], w["w_uv"], cache_kv, kpe_t, kinv_t)


def _wkv_kernel(*refs, tk, n_chunks, has_state):
    if has_state:
        (r_ref, k_ref, v_ref, ld_ref, a_ref, zb_ref, kk_ref, ka_ref, rk_ref, gg_ref, gb_ref, s0_ref,
         o_ref, so_ref, s_ref) = refs
    else:
        (r_ref, k_ref, v_ref, ld_ref, a_ref, zb_ref, kk_ref, ka_ref, rk_ref, gg_ref, gb_ref,
         o_ref, so_ref, s_ref) = refs
    C, L, G = WKV_CHUNK, GROUP_LANES, WKV_GROUP
    GC = G * C
    n_groups = RW_WIDTH // L
    ti = pl.program_id(1)
    ri = lax.broadcasted_iota(jnp.int32, (GC, L), 0)
    ci = lax.broadcasted_iota(jnp.int32, (GC, L), 1)
    blk = lambda n: (ri // n) == (ci // n)
    same_head = blk(RW_HEAD).astype(F32)
    t2 = lax.broadcasted_iota(jnp.int32, (C, GC), 0)
    s2 = lax.broadcasted_iota(jnp.int32, (C, GC), 1) % C
    strict2 = (s2 < t2).astype(F32)
    incl2 = (s2 <= t2).astype(F32)
    tri2 = jnp.concatenate([strict2, incl2], axis=0)
    blk2 = lambda n: (t2 // n) == (s2 // n)
    neg_m16 = -blk2(16).astype(F32)
    m32_off = (blk2(32) & ~blk2(16)).astype(F32)
    m64_off = (~blk2(32)).astype(F32)
    eye2 = (t2 == s2).astype(F32)
    tri = (lax.broadcasted_iota(jnp.int32, (C, C), 1) <= lax.broadcasted_iota(jnp.int32, (C, C), 0)).astype(BF16)
    ones_bd = same_head.astype(BF16)
    replicate = (lax.broadcasted_iota(jnp.int32, (RW_HEAD, L), 1) % RW_HEAD
                 == lax.broadcasted_iota(jnp.int32, (RW_HEAD, L), 0)).astype(BF16)

    def seg_sum(xs, split):
        n = xs[0].shape[0]
        if not split:
            y = _dot(jnp.concatenate(xs, axis=0), ones_bd)
            return [y[i * n:(i + 1) * n] for i in range(len(xs))]
        parts = [piece for x in xs for piece in _split2(x)]
        y = jnp.dot(jnp.concatenate(parts, axis=0), ones_bd, preferred_element_type=F32)
        return [y[2 * i * n:(2 * i + 1) * n] + y[(2 * i + 1) * n:(2 * i + 2) * n] for i in range(len(xs))]

    half_lanes = lax.broadcasted_iota(jnp.int32, (C, LANES), 1) // RW_HEAD
    half_mask = [(half_lanes == i).astype(BF16) for i in range(LANES // RW_HEAD)]
    zero_slab = jnp.zeros((C, LANES), BF16)

    def expand(x):
        xb = x.astype(BF16)
        blocks = []
        for h in range(G):
            slab, half = divmod(h * RW_HEAD, LANES)
            keep = xb[:, slab * LANES:(slab + 1) * LANES] * half_mask[half // RW_HEAD]
            blocks.append(jnp.concatenate([keep if s == slab else zero_slab for s in range(L // LANES)], axis=1))
        return jnp.concatenate(blocks, axis=0)

    stack = expand

    @pl.when(ti == 0)
    def _():
        if has_state:
            for n in range(s_ref.shape[0]):
                hi, mid, lo = _split3(s0_ref[n // n_groups, (n % n_groups) * L:(n % n_groups + 1) * L, :])
                s_ref[n] = (_dot(hi, replicate) + _dot(mid, replicate) + _dot(lo, replicate)) * same_head
        else:
            s_ref[...] = jnp.zeros(s_ref.shape, F32)

    def each(f, *cols):
        return [f(*args) for args in zip(*cols)]

    groups = list(range(n_groups))
    lanes = [slice(g * L, (g + 1) * L) for g in groups]
    cpi = 4 if n_chunks % 4 == 0 else 1

    def step(i, carry):
        if tk == C:
            rows = [(0, pl.ds(pl.multiple_of((i * cpi + j) * C, C), C)) for j in range(cpi)]
            load = lambda ref: [ref[0, rw, ln] for _, rw in rows for ln in lanes]
        else:
            rows = [(n, slice(0, tk)) for n in range(r_ref.shape[0])]
            load = lambda ref: [jnp.concatenate([ref[n, :, ln], jnp.zeros((C - tk, L), F32)], axis=0)
                                for n, _ in rows for ln in lanes]
        r, k, v, ld, a, zb = (load(ref) for ref in (r_ref, k_ref, v_ref, ld_ref, a_ref, zb_ref))
        k_k, k_a, r_k, gn_g, gn_b = ([ref[:, ln] for _ in rows for ln in lanes]
                                     for ref in (kk_ref, ka_ref, rk_ref, gg_ref, gb_ref))
        ld_split = each(_split2, ld)
        cum = each(lambda hl: _dot(tri, hl[0]) + _dot(tri, hl[1]), ld_split)
        cum_end = each(lambda x: x[C - 1:C, :], cum)
        kk = each(lambda x, y: x * y, k, k_k)
        norm = seg_sum(each(lambda x: x * x, kk), split=True)
        kk = each(lambda x, n: x * lax.rsqrt(jnp.maximum(n, 1e-24)), kk, norm)
        k = each(lambda x, a_, ka: x * (1.0 + (a_ - 1.0) * ka), k, a, k_a)
        b = each(lambda x, y: x * y, kk, a)
        grow = each(lambda x: jnp.exp(-x), cum)
        tail = each(lambda e, x: jnp.exp(e - x), cum_end, cum)
        ar = each(lambda kk_, x, l_, r_: jnp.concatenate([kk_ * jnp.exp(x - l_), r_ * jnp.exp(x)], axis=0).astype(BF16),
                  kk, cum, ld, r)
        bk = each(lambda b_, k_, g_: jnp.concatenate([stack(b_ * g_), stack(k_ * g_)], axis=0),
                  b, k, grow)
        gm = each(_dot_nt, ar, bk)
        vs = each(stack, v)
        l2b = each(lambda m: m[:C, :GC] * strict2, gm)
        kv_part = each(lambda m, vs_: _dot(m[:, GC:] * tri2, vs_), gm, vs)
        m2b = each(lambda m: (m[C:, :GC] * incl2).astype(BF16), gm)
        kb = each(lambda k_, b_, t_: jnp.concatenate([k_ * t_, b_ * t_], axis=0).astype(BF16), k, b, tail)
        decay_end = each(jnp.exp, cum_end)
        rkr = each(lambda r_, k_, rk: r_ * k_ * rk, r, k, r_k)
        p = each(lambda l_: l_ * neg_m16, l2b)
        x = each(lambda p_: eye2 + p_, p)
        p = each(lambda p_: _dot(p_, expand(p_)), p)
        for _ in range(2):
            t = each(lambda x_, p_: _dot(jnp.concatenate([x_, p_], axis=0), expand(p_)), x, p)
            x = each(lambda x_, t_: x_ + t_[:C], x, t)
            p = each(lambda t_: t_[C:], t)
        x = each(lambda x_, p_: x_ + _dot(x_, expand(p_)), x, p)
        for off in (m32_off, m64_off):
            y = each(lambda l_, x_: _dot(l_ * off, expand(x_)), l2b, x)
            x = each(lambda x_, y_: x_ - _dot(x_, expand(y_)), x, y)
        x2 = each(lambda x_: x_.astype(BF16), x)

        def state_part(part, slots):
            s_prev = [s_ref[n] for n in slots]
            ars = each(_dot_nt, ar[part], s_prev)
            rhs = each(lambda s_, kv_: s_[:C] + kv_[:C], ars, kv_part[part])
            u = each(lambda x_, rhs_: _dot(x_, stack(rhs_)), x2[part], rhs)
            out = each(lambda s_, kv_, m_, u_: s_[C:] + kv_[C:] - _dot(m_, stack(u_)), ars, kv_part[part], m2b[part], u)
            ds = each(lambda v_, u_, kb_: _dot(jnp.concatenate([v_, -u_], axis=0).T, kb_), v[part], u, kb[part])
            for n, s_, d_, ds_ in zip(slots, s_prev, decay_end[part], ds):
                s_ref[n] = s_ * d_ + ds_ * same_head
            return out

        if has_state:
            o = state_part(slice(None), list(range(len(ar))))
        else:
            o = []
            for j in range(len(rows)):
                o += state_part(slice(j * n_groups, (j + 1) * n_groups), groups)

        sums = seg_sum(o + rkr, split=False)
        mean, bonus = sums[:len(o)], sums[len(o):]
        dev = each(lambda x_, m: x_ - m * (1.0 / RW_HEAD), o, mean)
        var = seg_sum(each(lambda x_: x_ * x_, dev), split=False)
        for n, (seq, rw, ln) in enumerate((seq, rw, ln) for seq, rw in rows for ln in lanes):
            out = dev[n] * lax.rsqrt(var[n] * (1.0 / RW_HEAD) + GN_EPS) * gn_g[n] + gn_b[n] + bonus[n] * v[n]
            out = (out * _silu(zb[n])).astype(BF16)
            o_ref[seq, rw, ln] = out[:tk] if tk != C else out
        return carry

    if tk != C:
        step(0, 0)
    else:
        lax.fori_loop(0, n_chunks // cpi, step, 0)

    @pl.when(ti == pl.num_programs(1) - 1)
    def _():
        rep_t = (lax.broadcasted_iota(jnp.int32, (L, RW_HEAD), 0) % RW_HEAD
                 == lax.broadcasted_iota(jnp.int32, (L, RW_HEAD), 1)).astype(BF16)
        for n in range(s_ref.shape[0]):
            hi, mid, lo = _split3(s_ref[n])
            so_ref[n // n_groups, (n % n_groups) * L:(n % n_groups + 1) * L, :] = (
                _dot(hi, rep_t) + _dot(mid, rep_t) + _dot(lo, rep_t))


def _wkv(r, k, v, ld, a, zb, w, state):
    nb, t_all, _ = r.shape
    has_state = state is not None
    if has_state:
        tk, tb, bb = t_all, t_all, _tile(nb, WKV_SEQS)
    else:
        tk, tb, bb = WKV_CHUNK, _tile(t_all, 512), 1
    n_groups = RW_WIDTH // GROUP_LANES
    seq = pl.BlockSpec((bb, tb, RW_WIDTH), lambda b, t: (b, t, 0))
    vec = pl.BlockSpec((1, RW_WIDTH), lambda b, t: (0, 0))
    st = pl.BlockSpec((bb, RW_WIDTH, RW_HEAD), lambda b, t: (b, 0, 0))
    in_specs = [seq] * 6 + [vec] * 5 + ([st] if has_state else [])
    args = [r, k, v, ld, a, zb, w["k_k"], w["k_a"], w["r_k"], w["gn_g"], w["gn_b"]] + ([state] if has_state else [])
    return pl.pallas_call(
        functools.partial(_wkv_kernel, tk=tk, n_chunks=tb // tk, has_state=has_state),
        grid=(nb // bb, t_all // tb), in_specs=in_specs, out_specs=[seq, st],
        out_shape=[jax.ShapeDtypeStruct((nb, t_all, RW_WIDTH), BF16),
                   jax.ShapeDtypeStruct((nb, RW_WIDTH, RW_HEAD), F32)],
        scratch_shapes=[pltpu.VMEM((bb * n_groups, GROUP_LANES, GROUP_LANES), F32)],
        name="wkv", compiler_params=_params("parallel", "arbitrary"),
    )(*args)


def _merge_kernel(x_ref, mod_ref, oa_ref, ob_ref, ga_ref, gb_ref, wpa_ref, wpb_ref, wout_ref, y_ref, *, bb, tt):
    tm = bb * tt
    flat = lambda ref: ref[...].reshape(tm, ref.shape[2])
    m = (_sigmoid(flat(ga_ref)) * _dot(flat(oa_ref), wpa_ref[...])
         + _sigmoid(flat(gb_ref)) * _dot(flat(ob_ref), wpb_ref[...]))
    gate = mod_ref[...][:, :, 2 * D_MODEL:]
    y_ref[...] = x_ref[...] + gate * _dot(m, wout_ref[...]).reshape(bb, tt, D_MODEL)


def _merge(x, mod, oa, ob, ga, gb, w, *, bb, tt):
    nb, t_all, _ = x.shape
    tok = pl.BlockSpec((bb, tt, D_MODEL), lambda i, j: (i, j, 0))
    return pl.pallas_call(
        functools.partial(_merge_kernel, bb=bb, tt=tt),
        grid=(nb // bb, t_all // tt),
        in_specs=[tok, pl.BlockSpec((bb, 1, 3 * D_MODEL), lambda i, j: (i, 0, 0)), tok, tok, tok, tok,
                  _const_spec(w["w_pa"].shape), _const_spec(w["w_pb"].shape), _const_spec(w["w_out"].shape)],
        out_specs=tok, out_shape=jax.ShapeDtypeStruct(x.shape, F32),
        name="merge", compiler_params=_params("parallel", "parallel"),
    )(x, mod, oa, ob, ga, gb, w["w_pa"], w["w_pb"], w["w_out"])


def _head_slabs(rope_part, nope_part):
    lead = nope_part.shape[:-1]
    parts = []
    if rope_part is None:
        parts.append(jnp.zeros(lead + (NOPE_OFF,), nope_part.dtype))
    else:
        parts += [rope_part, jnp.zeros(lead + (NOPE_OFF - QK_ROPE,), nope_part.dtype)]
    parts.append(nope_part)
    slab = jnp.concatenate(parts, axis=-1)
    return slab.reshape(lead[:-1] + (MLA_HEADS * HEAD_PAD,))


def _pad_cols(x, width):
    return jnp.pad(x, ((0, 0), (0, width - x.shape[1])))


def _prepare_weights(w_ada, b_ada, g_norm, w_in, g_qa, w_uq, g_kva, w_uk, w_uv, g_q, g_k, w_pa, mu_shift, w0,
                     w_decay2, a0, w_a2, k_k, k_a, r_k, gn_g, gn_b, w_pb, w_out):
    bf = lambda x: x.astype(BF16)
    row = lambda x: x.reshape(1, -1)
    splits = (Q_LORA, KV_LORA, QK_ROPE, MLA_WIDTH, RW_WIDTH, DECAY_LORA, RW_WIDTH, RW_WIDTH, A_LORA, RW_WIDTH,
              D_MODEL, D_MODEL)
    offs = [0]
    for s in splits:
        offs.append(offs[-1] + s)
    cols = [w_in[:, offs[i]:offs[i + 1]] for i in range(len(splits))]
    w_qa, w_kva, w_kpe, w_za, w_r, w_wl, w_k, w_v, w_al, w_zb, w_ga, w_gb = cols
    mu = [mu_shift[offs[i] - offs[4]:offs[i + 1] - offs[4]] for i in range(4, 9)]
    pad_row = lambda x: _pad_cols(row(x), LANES).reshape(1, 1, LANES)
    row3 = lambda x: x.reshape(1, 1, -1)
    zeros_h = jnp.zeros((KV_LORA, MLA_HEADS, V_DIM), F32)
    parity = (jnp.arange(MLA_HEADS) % 2 == 0)[None, :, None]
    w_uv_slab = jnp.concatenate([jnp.where(parity, w_uv, zeros_h), jnp.where(parity, zeros_h, w_uv)], axis=-1)
    w_ukt = jnp.pad(jnp.transpose(w_uk, (1, 2, 0)), ((0, 0), (NOPE_OFF, 0), (0, 0)))
    swap_cols = lambda x: jnp.concatenate([-x[..., HALF_ROPE:], x[..., :HALF_ROPE]], axis=-1)
    swap_gain = lambda g: jnp.pad(jnp.concatenate([g[HALF_ROPE:], g[:HALF_ROPE]]), (0, HEAD_PAD - QK_ROPE))
    w_uq_swap = _head_slabs(swap_cols(w_uq[..., QK_NOPE:]), jnp.zeros_like(w_uq[..., :QK_NOPE]))
    w_kpe_both = jnp.concatenate([_pad_cols(w_kpe, HEAD_PAD), _pad_cols(swap_cols(w_kpe), HEAD_PAD)], axis=1)
    g_k_rope = jnp.pad(g_k[QK_NOPE:], (0, HEAD_PAD - QK_ROPE))
    g_k_nope = jnp.pad(g_k[:QK_NOPE], (NOPE_OFF, 0))
    g_q_slab = jnp.concatenate([g_q[QK_NOPE:], jnp.zeros((NOPE_OFF - QK_ROPE,), F32), g_q[:QK_NOPE]])
    return {
        "w_ada": bf(w_ada), "b_ada": row(b_ada), "g_norm": row3(g_norm),
        "w_qa": bf(w_qa), "w_kva": bf(w_kva), "w_kpe": bf(w_kpe_both), "w_za": bf(w_za),
        "w_ga": bf(w_ga), "w_gb": bf(w_gb), "g_qa": row(g_qa), "g_kva": row(g_kva),
        "w_uq": bf(_head_slabs(w_uq[..., QK_NOPE:], w_uq[..., :QK_NOPE])),
        "w_uk": bf(_head_slabs(None, w_uk)), "w_uv": bf(w_uv_slab.reshape(KV_LORA, QKV_WIDTH)),
        "w_uq_swap": bf(w_uq_swap), "w_ukt": bf(w_ukt), "g_q": row(g_q_slab),
        "g_q_swap": row(swap_gain(g_q[QK_NOPE:])), "w_uv_t": bf(w_uv.reshape(KV_LORA, MLA_WIDTH).T),
        "g_k_rope": row(g_k_rope), "g_k_rope_swap": row(swap_gain(g_k[QK_NOPE:])), "g_k_nope": row(g_k_nope),
        "w_r": bf(w_r), "w_wl": bf(_pad_cols(w_wl, LANES)), "w_k": bf(w_k), "w_v": bf(w_v),
        "w_al": bf(_pad_cols(w_al, LANES)), "w_zb": bf(w_zb),
        "mu_r": row3(mu[0]), "mu_w": pad_row(mu[1]), "mu_k": row3(mu[2]), "mu_v": row3(mu[3]), "mu_a": pad_row(mu[4]),
        "w_decay2": bf(jnp.pad(w_decay2, ((0, LANES - DECAY_LORA), (0, 0)))), "w0": row(w0),
        "w_a2": bf(jnp.pad(w_a2, ((0, LANES - A_LORA), (0, 0)))), "a0": row(a0),
        "k_k": row(k_k), "k_a": row(k_a), "r_k": row(r_k), "gn_g": row(gn_g), "gn_b": row(gn_b),
        "w_pa": bf(w_pa), "w_pb": bf(w_pb), "w_out": bf(w_out),
    }


def _rope_tables(pos):
    inv = ROPE_THETA ** (-jnp.arange(HALF_ROPE, dtype=F32) / HALF_ROPE)
    ang = pos.astype(F32)[:, None] * inv[None, :]
    cos, sin = jnp.cos(ang), jnp.sin(ang)
    n = pos.shape[0]
    rest = jnp.zeros((n, HEAD_PAD - QK_ROPE), F32)
    return jnp.concatenate([cos, cos, rest + 1.0], axis=1), jnp.concatenate([sin, sin, rest], axis=1)


def _split_shift(state_shift):
    o1 = RW_WIDTH
    o2 = o1 + DECAY_LORA
    o3 = o2 + RW_WIDTH
    o4 = o3 + RW_WIDTH
    n = state_shift.shape[0]
    pad = lambda x: _pad_cols(x, LANES)
    parts = (state_shift[:, :o1], pad(state_shift[:, o1:o2]), state_shift[:, o2:o3], state_shift[:, o3:o4],
             pad(state_shift[:, o4:]))
    return [p.reshape(n, 1, -1) for p in parts]


def _join_shift(last):
    lr, lw, lk, lv, la = (x[:, 0, :] for x in last)
    return jnp.concatenate([lr, lw[:, :DECAY_LORA], lk, lv, la[:, :A_LORA]], axis=1)


def _tile(n, target):
    t = min(n, target)
    while n % t:
        t //= 2
    return t


def kernel(x_prompt, x_sample, c_prompt, c_sample, cache_kv, cache_kpe, cache_kinv, state_wkv, state_shift, page_table, w_ada, b_ada, g_norm, w_in, g_qa, w_uq, g_kva, w_uk, w_uv, g_q, g_k, w_pa, mu_shift, w0, w_decay2, a0, w_a2, k_k, k_a, r_k, gn_g, gn_b, w_pb, w_out):
    B, S, _ = x_prompt.shape
    Bd, T, _ = x_sample.shape
    n_pages = page_table.shape[1]
    past = n_pages * PAGE_SIZE
    w = _prepare_weights(w_ada, b_ada, g_norm, w_in, g_qa, w_uq, g_kva, w_uk, w_uv, g_q, g_k, w_pa, mu_shift, w0,
                         w_decay2, a0, w_a2, k_k, k_a, r_k, gn_g, gn_b, w_pb, w_out)

    mod = _mod(jnp.concatenate([c_prompt, c_sample], axis=0), w["w_ada"], w["b_ada"])
    mod_p = mod[:B].reshape(B, 1, 3 * D_MODEL)
    mod_s = mod[B:].reshape(Bd, 1, 3 * D_MODEL)

    tt = _tile(S, 256)
    q, kv_p, kpe_p, kinv_p, za, ga, gb, keys, vals = _front_mla(
        x_prompt, mod_p, _rope_tables(jnp.arange(S)), w, SCALE * LOG2_E, bb=1, tt=tt, q_dtype=BF16, emit_kv=True)
    zero_shift = _split_shift(jnp.zeros((B, 3 * RW_WIDTH + DECAY_LORA + A_LORA), F32))
    r, k, v, ld, a, zb, *last_p = _front_rwkv(x_prompt, mod_p, zero_shift, w, bb=1, tt=_tile(S, 512))
    o_a = _attn_prompt(q, keys, vals, za, tq=_tile(S, 256), tk=_tile(S, 256))
    o_b, wkv_p = _wkv(r, k, v, ld, a, zb, w, None)
    y_prompt = _merge(x_prompt, mod_p, o_a, o_b, ga, gb, w, bb=1, tt=_tile(S, 512))

    bs = _tile(Bd, 32)
    q_s, kv_s, kpe_s, kinv_s, za_s, ga_s, gb_s = _front_mla(
        x_sample, mod_s, _rope_tables(past + jnp.arange(T)), w, SCALE, bb=bs, tt=T, q_dtype=F32, emit_kv=False)
    r_s, k_s, v_s, ld_s, a_s, zb_s, *last_s = _front_rwkv(x_sample, mod_s, _split_shift(state_shift), w, bb=bs, tt=T)
    o_a_s = _attn_sample(page_table, q_s, kv_s, kpe_s, kinv_s, za_s, cache_kv, cache_kpe, cache_kinv, w,
                         pp=_tile(n_pages, 32))
    o_b_s, wkv_s = _wkv(r_s, k_s, v_s, ld_s, a_s, zb_s, w, state_wkv.reshape(Bd, RW_WIDTH, RW_HEAD))
    y_sample = _merge(x_sample, mod_s, o_a_s, o_b_s, ga_s, gb_s, w, bb=bs, tt=T)

    return (y_prompt, y_sample, kv_p, kpe_p, kinv_p, wkv_p.reshape(B, RW_HEADS, RW_HEAD, RW_HEAD),
            _join_shift(last_p), kv_s, kpe_s, kinv_s, wkv_s.reshape(Bd, RW_HEADS, RW_HEAD, RW_HEAD),
            _join_shift(last_s))
```
